```python
import jax, jax.numpy as jnp
from jax import lax
import numpy as np

D_MODEL = 1024
BATCH = 4
SEQ = 8192
DEPTH = 2

N_META = 16
CONV_WIDTH = 3
HEAD_DIM = 64
N_Q_HEADS = D_MODEL // HEAD_DIM
N_KV_HEADS = N_Q_HEADS // 4
GROUP = N_Q_HEADS // N_KV_HEADS
Q_DIM = N_Q_HEADS * HEAD_DIM
KV_DIM = N_KV_HEADS * HEAD_DIM
QKV_DIM = Q_DIM + 2 * KV_DIM
WINDOW = 128
BLOCK = 128
ROPE_THETA = 10000.0
D_FF = 7 * D_MODEL // 2
N_EXPERTS = 8
TOP_K = 2
EXPERT_FF = 7 * D_MODEL // 2
RMS_EPS = 1e-5
N_CONV_LAYERS = (DEPTH + 1) // 2
N_ATTN_LAYERS = DEPTH // 2

kernel_name = "hybrid_shortconv_swa_sink_moe_block"


def rmsnorm(x, g):
    xf = x.astype(jnp.float32)
    y = xf * lax.rsqrt(jnp.mean(xf * xf, axis=-1, keepdims=True) + RMS_EPS)
    return (y * g.astype(jnp.float32)).astype(x.dtype)


def swiglu(h, w_gate, w_up, w_down):
    return (jax.nn.silu(h @ w_gate) * (h @ w_up)) @ w_down


def short_conv_mixer(h, w_in, conv_w, w_out):
    b_gate, c_gate, u = jnp.split(h @ w_in, 3, axis=-1)
    v = c_gate * u
    L = v.shape[1]
    vp = jnp.pad(v, ((0, 0), (CONV_WIDTH - 1, 0), (0, 0)))
    conv = sum(conv_w[k] * vp[:, k:k + L] for k in range(CONV_WIDTH))
    return (b_gate * conv) @ w_out


def rope_tables(L):
    pos = jnp.arange(L, dtype=jnp.float32)
    inv = ROPE_THETA ** (-jnp.arange(0, HEAD_DIM, 2, dtype=jnp.float32) / HEAD_DIM)
    ang = pos[:, None] * inv[None, :]
    return jnp.cos(ang), jnp.sin(ang)


def apply_rope(x, cos, sin):
    x1, x2 = jnp.split(x.astype(jnp.float32), 2, axis=-1)
    c = cos[None, :, None, :]
    s = sin[None, :, None, :]
    return jnp.concatenate([x1 * c - x2 * s, x2 * c + x1 * s], axis=-1).astype(x.dtype)


def sliding_window_attention(h, w_qkv, b_qkv, sinks, w_o):
    Bsz, L, _ = h.shape
    qkv = h @ w_qkv + b_qkv
    q, k, v = jnp.split(qkv, [Q_DIM, Q_DIM + KV_DIM], axis=-1)
    q = q.reshape(Bsz, L, N_Q_HEADS, HEAD_DIM)
    k = k.reshape(Bsz, L, N_KV_HEADS, HEAD_DIM)
    v = v.reshape(Bsz, L, N_KV_HEADS, HEAD_DIM)
    cos, sin = rope_tables(L)
    q = apply_rope(q, cos, sin)
    k = apply_rope(k, cos, sin)
    pad = (-L) % BLOCK
    L_pad = L + pad
    nb = L_pad // BLOCK
    padf = lambda t: jnp.pad(t, ((0, 0), (pad, 0), (0, 0), (0, 0)))
    qb = padf(q).reshape(Bsz, nb, BLOCK, N_KV_HEADS, GROUP, HEAD_DIM)
    kb = padf(k).reshape(Bsz, nb, BLOCK, N_KV_HEADS, HEAD_DIM)
    vb = padf(v).reshape(Bsz, nb, BLOCK, N_KV_HEADS, HEAD_DIM)
    prev = lambda t: jnp.concatenate([jnp.zeros_like(t[:, :1]), t[:, :-1]], axis=1)
    kw = jnp.concatenate([prev(kb), kb], axis=2)
    vw = jnp.concatenate([prev(vb), vb], axis=2)
    scale = HEAD_DIM ** -0.5
    s = jnp.einsum('bnqhgd,bnshd->bnhgqs', qb, kw, preferred_element_type=jnp.float32) * scale
    r = jnp.arange(BLOCK)[:, None]
    j = jnp.arange(2 * BLOCK)[None, :]
    diff = BLOCK + r - j
    k_abs = jnp.arange(nb)[:, None, None] * BLOCK - BLOCK + j[None]
    mask = ((diff >= 0) & (diff < WINDOW))[None] & (k_abs >= pad)
    s = jnp.where(mask[None, :, None, None], s, -jnp.inf)
    sink = sinks.astype(jnp.float32).reshape(1, 1, N_KV_HEADS, GROUP, 1, 1)
    m = jnp.maximum(jnp.max(s, axis=-1, keepdims=True), sink)
    p = jnp.exp(s - m)
    denom = jnp.sum(p, axis=-1, keepdims=True) + jnp.exp(sink - m)
    p = (p / denom).astype(vw.dtype)
    o = jnp.einsum('bnhgqs,bnshd->bnqhgd', p, vw)
    o = o.reshape(Bsz, L_pad, Q_DIM)[:, pad:]
    return o @ w_o


def moe_swiglu(h, w_router, w_gate, w_up, w_down):
    logits = jnp.einsum('bld,de->ble', h, w_router).astype(jnp.float32)
    top_val, top_idx = lax.top_k(logits, TOP_K)
    gates = jax.nn.softmax(top_val, axis=-1)
    combine = jnp.sum(jax.nn.one_hot(top_idx, N_EXPERTS, dtype=jnp.float32) * gates[..., None], axis=-2)
    out = jnp.zeros_like(h)
    for e in range(N_EXPERTS):
        out = out + combine[..., e:e + 1].astype(h.dtype) * swiglu(h, w_gate[e], w_up[e], w_down[e])
    return out


def setup_inputs(seed: int = 0) -> dict:
    key = jax.random.key(seed)
    ks = iter(jax.random.split(key, 32))
    nrm = lambda shape, scale: jax.random.normal(next(ks), shape, jnp.float32) * scale
    gain = lambda shape: 1.0 + nrm(shape, 0.02)
    D, NA, NB = D_MODEL, N_CONV_LAYERS, N_ATTN_LAYERS
    return {
        "x": nrm((BATCH, SEQ, D), 1.0),
        "meta_tokens": nrm((N_META, D), 1.0),
        "conv_norm": gain((NA, D)),
        "conv_w_in": nrm((NA, D, 3 * D), D ** -0.5),
        "conv_w": nrm((NA, CONV_WIDTH, D), CONV_WIDTH ** -0.5),
        "conv_w_out": nrm((NA, D, D), D ** -0.5),
        "ffn_norm": gain((NA, D)),
        "ffn_w_gate": nrm((NA, D, D_FF), D ** -0.5),
        "ffn_w_up": nrm((NA, D, D_FF), D ** -0.5),
        "ffn_w_down": nrm((NA, D_FF, D), D_FF ** -0.5),
        "attn_norm": gain((NB, D)),
        "attn_w_qkv": nrm((NB, D, QKV_DIM), D ** -0.5),
        "attn_b_qkv": nrm((NB, QKV_DIM), 0.02),
        "attn_sinks": nrm((NB, N_Q_HEADS), 0.5),
        "attn_w_o": nrm((NB, Q_DIM, D), Q_DIM ** -0.5),
        "moe_norm": gain((NB, D)),
        "moe_w_router": nrm((NB, D, N_EXPERTS), D ** -0.5),
        "moe_w_gate": nrm((NB, N_EXPERTS, D, EXPERT_FF), D ** -0.5),
        "moe_w_up": nrm((NB, N_EXPERTS, D, EXPERT_FF), D ** -0.5),
        "moe_w_down": nrm((NB, N_EXPERTS, EXPERT_FF, D), EXPERT_FF ** -0.5),
        "final_norm": gain((D,)),
    }


def reference(x, meta_tokens, conv_norm, conv_w_in, conv_w, conv_w_out, ffn_norm, ffn_w_gate, ffn_w_up,
              ffn_w_down, attn_norm, attn_w_qkv, attn_b_qkv, attn_sinks, attn_w_o, moe_norm, moe_w_router,
              moe_w_gate, moe_w_up, moe_w_down, final_norm):
    Bsz = x.shape[0]
    meta = jnp.broadcast_to(meta_tokens[None].astype(x.dtype), (Bsz, N_META, D_MODEL))
    h = jnp.concatenate([meta, x], axis=1)
    for i in range(DEPTH):
        j = i // 2
        if i % 2 == 0:
            h = h + short_conv_mixer(rmsnorm(h, conv_norm[j]), conv_w_in[j], conv_w[j], conv_w_out[j])
            h = h + swiglu(rmsnorm(h, ffn_norm[j]), ffn_w_gate[j], ffn_w_up[j], ffn_w_down[j])
        else:
            h = h + sliding_window_attention(rmsnorm(h, attn_norm[j]), attn_w_qkv[j], attn_b_qkv[j],
                                             attn_sinks[j], attn_w_o[j])
            h = h + moe_swiglu(rmsnorm(h, moe_norm[j]), moe_w_router[j], moe_w_gate[j], moe_w_up[j],
                               moe_w_down[j])
    return rmsnorm(h, final_norm)[:, N_META:]
```

```python
import functools

import jax
import jax.numpy as jnp
from jax import lax
from jax.experimental import pallas as pl
from jax.experimental.pallas import tpu as pltpu

CONV_WIDTH = 3
HEAD_DIM = 64
GROUP = 4
WINDOW = 128
BLOCK = 128
ROPE_THETA = 10000.0
N_EXPERTS = 8
RMS_EPS = 1e-5
NEG_BIG = -1e30

V7X_LANES = 128
V7X_SUBLANES = 8
VMEM_LIMIT = 56 * 1024 * 1024

BF16 = jnp.bfloat16
F32 = jnp.float32


def _dot(a, b):
    return jnp.dot(a, b, preferred_element_type=F32)


def _rmsnorm(x, g):
    ms = jnp.mean(x * x, axis=-1, keepdims=True)
    return x * lax.rsqrt(ms + RMS_EPS) * g


def _silu(g):
    return g * (1.0 / (1.0 + jnp.exp(-g)))


def _const_spec(shape):
    nd = len(shape)
    return pl.BlockSpec(shape, lambda *_: (0,) * nd, pipeline_mode=pl.Buffered(1))


def _params(sem):
    return pltpu.CompilerParams(dimension_semantics=sem, vmem_limit_bytes=VMEM_LIMIT)


def _conv_mixer_kernel(h_ref, carry0_ref, g_ref, win_ref, cw_ref, wout_ref, out_ref, vtail_ref,
                       carry_ref, *, tiles_per_seq):
    d = h_ref.shape[1]
    tm = h_ref.shape[0]

    @pl.when(pl.program_id(0) % tiles_per_seq == 0)
    def _():
        carry_ref[...] = carry0_ref[...]

    h = h_ref[...]
    xn = _rmsnorm(h, g_ref[...]).astype(BF16)
    b_gate = _dot(xn, win_ref[:, 0:d])
    c_gate = _dot(xn, win_ref[:, d:2 * d])
    u = _dot(xn, win_ref[:, 2 * d:3 * d])
    v = c_gate * u
    carry = carry_ref[...]
    row = lax.broadcasted_iota(jnp.int32, (V7X_SUBLANES, d), 0)
    r1 = pltpu.roll(v, 1, axis=0)
    r2 = pltpu.roll(v, 2, axis=0)
    head1 = jnp.where(row < 1, pltpu.roll(carry, 1, axis=0), r1[0:V7X_SUBLANES])
    head2 = jnp.where(row < 2, pltpu.roll(carry, 2, axis=0), r2[0:V7X_SUBLANES])
    v1 = jnp.concatenate([head1, r1[V7X_SUBLANES:]], axis=0)
    v2 = jnp.concatenate([head2, r2[V7X_SUBLANES:]], axis=0)
    conv = cw_ref[0:1, :] * v2 + cw_ref[1:2, :] * v1 + cw_ref[2:3, :] * v
    y = (b_gate * conv).astype(BF16)
    out_ref[...] = h + _dot(y, wout_ref[...])
    tail = v[tm - V7X_SUBLANES:tm]
    carry_ref[...] = tail
    vtail_ref[...] = tail


def _conv_mixer(h, carry0, g, w_in, conv_w, w_out, *, tm, tiles_per_seq):
    t, d = h.shape
    nt = t // tm
    return pl.pallas_call(
        functools.partial(_conv_mixer_kernel, tiles_per_seq=tiles_per_seq),
        grid=(nt,),
        in_specs=[
            pl.BlockSpec((tm, d), lambda i: (i, 0)),
            _const_spec((V7X_SUBLANES, d)),
            _const_spec((1, d)),
            _const_spec((d, 3 * d)),
            _const_spec((CONV_WIDTH, d)),
            _const_spec((d, d)),
        ],
        out_specs=[
            pl.BlockSpec((tm, d), lambda i: (i, 0)),
            pl.BlockSpec((V7X_SUBLANES, d), lambda i: (i, 0)),
        ],
        out_shape=[
            jax.ShapeDtypeStruct((t, d), F32),
            jax.ShapeDtypeStruct((nt * V7X_SUBLANES, d), F32),
        ],
        scratch_shapes=[pltpu.VMEM((V7X_SUBLANES, d), F32)],
        compiler_params=_params(("arbitrary",)),
        name="conv_mixer",
    )(h, carry0, g, w_in, conv_w, w_out)


def _ff_chunks(total, chunk):
    out, s = [], 0
    while s < total:
        out.append((s, min(chunk, total - s)))
        s += chunk
    return out


def _swiglu_partial(x, wg_ref, wu_ref, wd_ref, chunk):
    acc = None
    for s, n in _ff_chunks(wg_ref.shape[1], chunk):
        g = _dot(x, wg_ref[:, s:s + n])
        u = _dot(x, wu_ref[:, s:s + n])
        a = (_silu(g) * u).astype(BF16)
        part = _dot(a, wd_ref[s:s + n, :])
        acc = part if acc is None else acc + part
    return acc


def _ffn_kernel(h_ref, g_ref, wg_ref, wu_ref, wd_ref, out_ref, *, chunk):
    h = h_ref[...]
    xn = _rmsnorm(h, g_ref[...]).astype(BF16)
    out_ref[...] = h + _swiglu_partial(xn, wg_ref, wu_ref, wd_ref, chunk)


def _ffn(h, g, wg, wu, wd, *, tm, chunk=512):
    t, d = h.shape
    f = wg.shape[1]
    return pl.pallas_call(
        functools.partial(_ffn_kernel, chunk=chunk),
        grid=(t // tm,),
        in_specs=[
            pl.BlockSpec((tm, d), lambda i: (i, 0)),
            _const_spec((1, d)),
            _const_spec((d, f)),
            _const_spec((d, f)),
            _const_spec((f, d)),
        ],
        out_specs=pl.BlockSpec((tm, d), lambda i: (i, 0)),
        out_shape=jax.ShapeDtypeStruct((t, d), F32),
        compiler_params=_params(("arbitrary",)),
        name="dense_ffn",
    )(h, g, wg, wu, wd)


def _rope(x, cos, sin_signed, first_half):
    outs = []
    for j in range(x.shape[1] // V7X_LANES):
        xj = x[:, j * V7X_LANES:(j + 1) * V7X_LANES]
        partner = jnp.where(first_half, pltpu.roll(xj, V7X_LANES - HEAD_DIM // 2, axis=1),
                            pltpu.roll(xj, HEAD_DIM // 2, axis=1))
        outs.append(xj * cos + partner * sin_signed)
    return jnp.concatenate(outs, axis=1)


def _qkv_kernel(h_ref, g_ref, w_ref, b_ref, cos_ref, sin_ref, q_ref, k_ref, v_ref):
    d = h_ref.shape[1]
    xn = _rmsnorm(h_ref[...], g_ref[...]).astype(BF16)
    cos = cos_ref[...]
    sin_signed = sin_ref[...]
    lane = lax.broadcasted_iota(jnp.int32, cos.shape, 1)
    first_half = (lane % HEAD_DIM) < (HEAD_DIM // 2)
    q = _dot(xn, w_ref[:, 0:d]) + b_ref[:, 0:d]
    k = _dot(xn, w_ref[:, d:2 * d]) + b_ref[:, d:2 * d]
    v = _dot(xn, w_ref[:, 2 * d:3 * d]) + b_ref[:, 2 * d:3 * d]
    q_ref[...] = (_rope(q, cos, sin_signed, first_half) * (HEAD_DIM ** -0.5)).astype(BF16)
    k_ref[...] = _rope(k, cos, sin_signed, first_half).astype(BF16)
    v_ref[...] = v.astype(BF16)


def _qkv_rope(h, g, w_exp, b_exp, cos, sin_signed, *, tm, tiles_per_seq):
    t, d = h.shape
    tok = pl.BlockSpec((tm, d), lambda i: (i, 0))
    tab = pl.BlockSpec((tm, V7X_LANES), lambda i: (i % tiles_per_seq, 0))
    return pl.pallas_call(
        _qkv_kernel,
        grid=(t // tm,),
        in_specs=[tok, _const_spec((1, d)), _const_spec((d, 3 * d)), _const_spec((1, 3 * d)), tab, tab],
        out_specs=[tok, tok, tok],
        out_shape=[jax.ShapeDtypeStruct((t, d), BF16)] * 3,
        compiler_params=_params(("arbitrary",)),
        name="qkv_rope",
    )(h, g, w_exp, b_exp, cos, sin_signed)


def _attn_kernel(sinks_ref, q_ref, k_ref, v_ref, k0_ref, v0_ref, h_ref, wo_ref, out_ref,
                 kprev_ref, vprev_ref, o_ref, *, tiles_per_seq, first_block_min_key):
    tq, d = q_ref.shape
    nblk = tq // BLOCK
    n_kv = d // (GROUP * HEAD_DIM)
    hw = GROUP * HEAD_DIM
    first = (pl.program_id(0) % tiles_per_seq) == 0

    @pl.when(first)
    def _():
        kprev_ref[...] = k0_ref[...]
        vprev_ref[...] = v0_ref[...]

    jmin = jnp.where(first, first_block_min_key, 0)
    rows = GROUP * BLOCK
    r_idx = lax.broadcasted_iota(jnp.int32, (rows, 2 * BLOCK), 0) % BLOCK
    j_idx = lax.broadcasted_iota(jnp.int32, (rows, 2 * BLOCK), 1)
    band = (j_idx > r_idx) & (j_idx <= r_idx + WINDOW)
    band_first = band & (j_idx >= jmin)
    row_grp = lax.broadcasted_iota(jnp.int32, (rows, 1), 0) // BLOCK
    lane_grp = lax.broadcasted_iota(jnp.int32, (BLOCK, hw), 1) // HEAD_DIM
    grp_mask = [lane_grp == g for g in range(GROUP)]
    grp_mask_bf = [m.astype(BF16) for m in grp_mask]

    for b in range(nblk):
        cur = slice(b * BLOCK, (b + 1) * BLOCK)
        if b == 0:
            kp, vp, valid = kprev_ref[...], vprev_ref[...], band_first
        else:
            prev = slice((b - 1) * BLOCK, b * BLOCK)
            kp, vp, valid = k_ref[prev, :], v_ref[prev, :], band
        kw = jnp.concatenate([kp, k_ref[cur, :]], axis=0)
        vw = jnp.concatenate([vp, v_ref[cur, :]], axis=0)
        qb = q_ref[cur, :]
        outs = []
        for hh in range(n_kv):
            ls = slice(hh * hw, (hh + 1) * hw)
            q_h, kw_h, vw_h = qb[:, ls], kw[:, ls], vw[:, ls]
            qs = jnp.concatenate([q_h * grp_mask_bf[g] for g in range(GROUP)], axis=0)
            s = lax.dot_general(qs, kw_h, (((1,), (1,)), ((), ())), preferred_element_type=F32)
            s = jnp.where(valid, s, NEG_BIG)
            sink = jnp.zeros((rows, 1), F32)
            for g in range(GROUP):
                sink = jnp.where(row_grp == g, sinks_ref[hh * GROUP + g], sink)
            m = jnp.maximum(jnp.max(s, axis=-1, keepdims=True), sink)
            p = jnp.exp(s - m)
            denom = jnp.sum(p, axis=-1, keepdims=True) + jnp.exp(sink - m)
            inv = 1.0 / denom
            pb = p.astype(BF16)
            o_h = jnp.zeros((BLOCK, hw), F32)
            for g in range(GROUP):
                rs = slice(g * BLOCK, (g + 1) * BLOCK)
                o_h = jnp.where(grp_mask[g], _dot(pb[rs], vw_h) * inv[rs], o_h)
            outs.append(o_h)
        o_ref[cur, :] = jnp.concatenate(outs, axis=1).astype(BF16)

    last = slice((nblk - 1) * BLOCK, nblk * BLOCK)
    kprev_ref[...] = k_ref[last, :]
    vprev_ref[...] = v_ref[last, :]
    out_ref[...] = h_ref[...] + _dot(o_ref[...], wo_ref[...])


def _attention(sinks, q, k, v, k0, v0, h, w_o, *, tq, tiles_per_seq, first_block_min_key):
    t, d = h.shape
    tok = pl.BlockSpec((tq, d), lambda i: (i, 0))
    return pl.pallas_call(
        functools.partial(_attn_kernel, tiles_per_seq=tiles_per_seq,
                          first_block_min_key=first_block_min_key),
        grid=(t // tq,),
        in_specs=[
            pl.BlockSpec(memory_space=pltpu.SMEM),
            tok, tok, tok,
            _const_spec((BLOCK, d)), _const_spec((BLOCK, d)),
            tok,
            _const_spec((d, d)),
        ],
        out_specs=tok,
        out_shape=jax.ShapeDtypeStruct((t, d), F32),
        scratch_shapes=[pltpu.VMEM((BLOCK, d), BF16), pltpu.VMEM((BLOCK, d), BF16),
                        pltpu.VMEM((tq, d), BF16)],
        compiler_params=_params(("arbitrary",)),
        name="swa_attention",
    )(sinks, q, k, v, k0, v0, h, w_o)


def _to_row_tiles(dst_ref, x):
    tm = x.shape[0]
    for c in range(x.shape[1] // V7X_LANES):
        dst_ref[pl.ds(c, tm, stride=V7X_SUBLANES), :] = x[:, c * V7X_LANES:(c + 1) * V7X_LANES]


def _from_row_tiles(src_ref, tm):
    n = src_ref.shape[0] // tm
    return jnp.concatenate([src_ref[pl.ds(c, tm, stride=n), :] for c in range(n)], axis=1)


def _router_kernel(h_ref, g_ref, wr_ref, xt_ref, route_ref):
    xn = _rmsnorm(h_ref[...], g_ref[...])
    x_hi = xn.astype(BF16)
    x_lo = (xn - x_hi.astype(F32)).astype(BF16)
    w = wr_ref[...]
    w_hi = w.astype(BF16)
    w_lo = (w - w_hi.astype(F32)).astype(BF16)
    logits = _dot(x_hi, w_hi) + (_dot(x_hi, w_lo) + _dot(x_lo, w_hi))
    lane = lax.broadcasted_iota(jnp.int32, logits.shape, 1)
    neg_inf = jnp.float32(-jnp.inf)
    lg = jnp.where(lane < N_EXPERTS, logits, neg_inf)
    m1 = jnp.max(lg, axis=-1, keepdims=True)
    i1 = jnp.min(jnp.where(lg == m1, lane, V7X_LANES), axis=-1, keepdims=True)
    lg2 = jnp.where(lane == i1, neg_inf, lg)
    m2 = jnp.max(lg2, axis=-1, keepdims=True)
    i2 = jnp.min(jnp.where(lg2 == m2, lane, V7X_LANES), axis=-1, keepdims=True)
    e = jnp.exp(m2 - m1)
    g1 = 1.0 / (1.0 + e)
    g2 = e / (1.0 + e)
    route = jnp.where(lane == 0, i1.astype(F32),
                      jnp.where(lane == 1, i2.astype(F32),
                                jnp.where(lane == 2, g1, jnp.where(lane == 3, g2, 0.0))))
    route_ref[...] = route
    _to_row_tiles(xt_ref, xn)


def _router(h, g, w_router_padded, *, tm):
    t, d = h.shape
    rt = d // V7X_LANES
    return pl.pallas_call(
        _router_kernel,
        grid=(t // tm,),
        in_specs=[pl.BlockSpec((tm, d), lambda i: (i, 0)), _const_spec((1, d)),
                  _const_spec((d, V7X_LANES))],
        out_specs=[pl.BlockSpec((tm * rt, V7X_LANES), lambda i: (i, 0)),
                   pl.BlockSpec((tm, V7X_LANES), lambda i: (i, 0))],
        out_shape=[jax.ShapeDtypeStruct((t * rt, V7X_LANES), F32),
                   jax.ShapeDtypeStruct((t, V7X_LANES), F32)],
        compiler_params=_params(("arbitrary",)),
        name="moe_router",
    )(h, g, w_router_padded)


def _gather_rows(idx_ref, src_hbm, dst_ref, sem, n_rows, rt):
    def issue(r, carry):
        src = pl.multiple_of(idx_ref[r] * rt, rt)
        dst = pl.multiple_of(r * rt, rt)
        pltpu.make_async_copy(src_hbm.at[pl.ds(src, rt), :], dst_ref.at[pl.ds(dst, rt), :], sem).start()
        return carry

    lax.fori_loop(0, n_rows, issue, 0)
    pltpu.make_async_copy(src_hbm.at[pl.ds(0, n_rows * rt), :], dst_ref, sem).wait()


def _expert_kernel(tile_expert_ref, n_used_ref, rows_ref, xt_hbm, wg_ref, wu_ref, wd_ref, out_ref,
                   gbuf_ref, xb_ref, acc_ref, sem, *, chunk):
    del tile_expert_ref
    tm, d = xb_ref.shape
    rt = d // V7X_LANES
    i, j = pl.program_id(0), pl.program_id(1)

    @pl.when(i < n_used_ref[0])
    def _():
        @pl.when(j == 0)
        def _():
            _gather_rows(rows_ref, xt_hbm, gbuf_ref, sem, tm, rt)
            xb_ref[...] = _from_row_tiles(gbuf_ref, tm).astype(BF16)

        part = _swiglu_partial(xb_ref[...], wg_ref, wu_ref, wd_ref, chunk)

        @pl.when(j == 0)
        def _():
            acc_ref[...] = part

        @pl.when(j > 0)
        def _():
            acc_ref[...] += part

        @pl.when(j == pl.num_programs(1) - 1)
        def _():
            _to_row_tiles(out_ref, acc_ref[...])

    @pl.when((i >= n_used_ref[0]) & (j == 0))
    def _():
        out_ref[...] = jnp.zeros_like(out_ref)


def _experts(tile_expert, n_used, row_src, xt, wg, wu, wd, *, tm, nf, chunk=512):
    n_e, d, f = wg.shape
    tf = f // nf
    rt = d // V7X_LANES
    s_pad = row_src.shape[0]
    n_tiles = s_pad // tm
    grid_spec = pltpu.PrefetchScalarGridSpec(
        num_scalar_prefetch=2,
        grid=(n_tiles, nf),
        in_specs=[
            pl.BlockSpec((tm,), lambda i, j, te, nu: (i,), memory_space=pltpu.SMEM),
            pl.BlockSpec(memory_space=pl.ANY),
            pl.BlockSpec((None, d, tf), lambda i, j, te, nu: (te[i], 0, j)),
            pl.BlockSpec((None, d, tf), lambda i, j, te, nu: (te[i], 0, j)),
            pl.BlockSpec((None, tf, d), lambda i, j, te, nu: (te[i], j, 0)),
        ],
        out_specs=pl.BlockSpec((tm * rt, V7X_LANES), lambda i, j, te, nu: (i, 0)),
        scratch_shapes=[pltpu.VMEM((tm * rt, V7X_LANES), F32), pltpu.VMEM((tm, d), BF16),
                        pltpu.VMEM((tm, d), F32), pltpu.SemaphoreType.DMA],
    )
    return pl.pallas_call(
        functools.partial(_expert_kernel, chunk=chunk),
        grid_spec=grid_spec,
        out_shape=jax.ShapeDtypeStruct((s_pad * rt, V7X_LANES), F32),
        compiler_params=_params(("arbitrary", "arbitrary")),
        name="moe_experts",
    )(tile_expert, n_used, row_src, xt, wg, wu, wd)


def _combine_kernel(pos1_ref, pos2_ref, ys_hbm, route_ref, h_ref, g_ref, out_ref, y1_ref, y2_ref, sem1, sem2):
    tm, d = h_ref.shape
    rt = d // V7X_LANES
    _gather_rows(pos1_ref, ys_hbm, y1_ref, sem1, tm, rt)
    _gather_rows(pos2_ref, ys_hbm, y2_ref, sem2, tm, rt)
    route = route_ref[...]
    g1, g2 = route[:, 2:3], route[:, 3:4]
    h = h_ref[...] + g1 * _from_row_tiles(y1_ref, tm) + g2 * _from_row_tiles(y2_ref, tm)
    out_ref[...] = _rmsnorm(h, g_ref[...])


def _combine(pos1, pos2, ys, route, h, g, *, tm):
    t, d = h.shape
    rt = d // V7X_LANES
    idx = pl.BlockSpec((tm,), lambda i: (i,), memory_space=pltpu.SMEM)
    return pl.pallas_call(
        _combine_kernel,
        grid=(t // tm,),
        in_specs=[idx, idx, pl.BlockSpec(memory_space=pl.ANY),
                  pl.BlockSpec((tm, V7X_LANES), lambda i: (i, 0)),
                  pl.BlockSpec((tm, d), lambda i: (i, 0)), _const_spec((1, d))],
        out_specs=pl.BlockSpec((tm, d), lambda i: (i, 0)),
        out_shape=jax.ShapeDtypeStruct((t, d), F32),
        scratch_shapes=[pltpu.VMEM((tm * rt, V7X_LANES), F32), pltpu.VMEM((tm * rt, V7X_LANES), F32),
                        pltpu.SemaphoreType.DMA, pltpu.SemaphoreType.DMA],
        compiler_params=_params(("arbitrary",)),
        name="moe_combine_norm",
    )(pos1, pos2, ys, route, h, g)


def _routing_plan(e1, e2, tm):
    t = e1.shape[0]
    flat_e = jnp.stack([e1, e2], axis=1).reshape(-1)
    onehot = (flat_e[:, None] == jnp.arange(N_EXPERTS, dtype=jnp.int32)[None, :]).astype(jnp.int32)
    csum = jnp.cumsum(onehot, axis=0)
    counts = csum[-1]
    rank = jnp.sum(onehot * csum, axis=1) - 1
    padded = ((counts + tm - 1) // tm) * tm
    ends = jnp.cumsum(padded)
    offs = ends - padded
    pos = jnp.sum(onehot * offs[None, :], axis=1) + rank
    n_tiles = (2 * t) // tm + N_EXPERTS
    s_pad = n_tiles * tm
    n_used = (ends[-1] // tm).astype(jnp.int32)
    tile_start = jnp.arange(n_tiles, dtype=jnp.int32) * tm
    tile_expert = jnp.sum((tile_start[:, None] >= ends[None, :]).astype(jnp.int32), axis=1)
    last_used_expert = jnp.sum((ends[-1] - 1 >= ends).astype(jnp.int32))
    tile_expert = jnp.minimum(tile_expert, last_used_expert).astype(jnp.int32)
    row_src = jnp.zeros((s_pad,), jnp.int32).at[pos].set(jnp.arange(2 * t, dtype=jnp.int32) // 2)
    pos = pos.reshape(t, 2).astype(jnp.int32)
    return tile_expert, n_used.reshape(1), row_src, pos[:, 0], pos[:, 1]


def _rope_tables(length):
    pos = jnp.arange(length, dtype=F32)
    inv = ROPE_THETA ** (-jnp.arange(0, HEAD_DIM, 2, dtype=F32) / HEAD_DIM)
    ang = pos[:, None] * inv[None, :]
    cos, sin = jnp.cos(ang), jnp.sin(ang)
    reps = V7X_LANES // HEAD_DIM
    cos_l = jnp.tile(jnp.concatenate([cos, cos], axis=1), (1, reps))
    sin_l = jnp.tile(jnp.concatenate([-sin, sin], axis=1), (1, reps))
    return cos_l, sin_l


def _expand_kv_cols(w, n_kv):
    lead = w.shape[:-1]
    w = w.reshape(lead + (n_kv, 1, HEAD_DIM))
    w = jnp.broadcast_to(w, lead + (n_kv, GROUP, HEAD_DIM))
    return w.reshape(lead + (n_kv * GROUP * HEAD_DIM,))


def kernel(x, meta_tokens, conv_norm, conv_w_in, conv_w, conv_w_out, ffn_norm, ffn_w_gate, ffn_w_up, ffn_w_down, attn_norm, attn_w_qkv, attn_b_qkv, attn_sinks, attn_w_o, moe_norm, moe_w_router, moe_w_gate, moe_w_up, moe_w_down, final_norm):
    bsz, seq, d = x.shape
    n_meta = meta_tokens.shape[0]
    depth = conv_norm.shape[0] + attn_norm.shape[0]
    assert depth == 2 and conv_norm.shape[0] == 1 and attn_norm.shape[0] == 1
    assert n_meta % V7X_SUBLANES == 0 and n_meta <= BLOCK and n_meta >= CONV_WIDTH - 1
    q_dim = d
    kv_dim = (attn_w_qkv.shape[2] - q_dim) // 2
    n_kv = kv_dim // HEAD_DIM
    assert q_dim == n_kv * GROUP * HEAD_DIM

    tm = 512
    assert seq % tm == 0 and tm % BLOCK == 0
    tiles_per_seq = seq // tm
    t = bsz * seq
    row = lambda a: a.reshape(1, -1).astype(F32)

    h = x.reshape(t, d)
    hm = meta_tokens.astype(F32)

    w_in = conv_w_in[0].astype(BF16)
    w_out = conv_w_out[0].astype(BF16)
    zero_carry = jnp.zeros((V7X_SUBLANES, d), F32)
    hm, vtail_m = _conv_mixer(hm, zero_carry, row(conv_norm[0]), w_in, conv_w[0], w_out,
                              tm=n_meta, tiles_per_seq=1)
    h, _ = _conv_mixer(h, vtail_m, row(conv_norm[0]), w_in, conv_w[0], w_out,
                       tm=tm, tiles_per_seq=tiles_per_seq)
    wg, wu, wd = ffn_w_gate[0].astype(BF16), ffn_w_up[0].astype(BF16), ffn_w_down[0].astype(BF16)
    hm = _ffn(hm, row(ffn_norm[0]), wg, wu, wd, tm=n_meta)
    h = _ffn(h, row(ffn_norm[0]), wg, wu, wd, tm=tm)

    w_qkv, b_qkv = attn_w_qkv[0], attn_b_qkv[0]
    w_exp = jnp.concatenate([w_qkv[:, :q_dim],
                             _expand_kv_cols(w_qkv[:, q_dim:q_dim + kv_dim], n_kv),
                             _expand_kv_cols(w_qkv[:, q_dim + kv_dim:], n_kv)], axis=1).astype(BF16)
    b_exp = jnp.concatenate([b_qkv[:q_dim],
                             _expand_kv_cols(b_qkv[q_dim:q_dim + kv_dim], n_kv),
                             _expand_kv_cols(b_qkv[q_dim + kv_dim:], n_kv)]).reshape(1, -1).astype(F32)
    cos_l, sin_l = _rope_tables(n_meta + seq)
    _, km, vm = _qkv_rope(hm, row(attn_norm[0]), w_exp, b_exp, cos_l[:n_meta], sin_l[:n_meta],
                          tm=n_meta, tiles_per_seq=1)
    q, k, v = _qkv_rope(h, row(attn_norm[0]), w_exp, b_exp, cos_l[n_meta:], sin_l[n_meta:],
                        tm=tm, tiles_per_seq=tiles_per_seq)
    k0 = jnp.zeros((BLOCK, d), BF16).at[BLOCK - n_meta:].set(km)
    v0 = jnp.zeros((BLOCK, d), BF16).at[BLOCK - n_meta:].set(vm)
    h = _attention(attn_sinks[0].astype(F32), q, k, v, k0, v0, h, attn_w_o[0].astype(BF16),
                   tq=tm, tiles_per_seq=tiles_per_seq, first_block_min_key=BLOCK - n_meta)

    w_r = jnp.zeros((d, V7X_LANES), F32).at[:, :N_EXPERTS].set(moe_w_router[0].astype(F32))
    xt, route = _router(h, row(moe_norm[0]), w_r, tm=tm)
    e1 = route[:, 0].astype(jnp.int32)
    e2 = route[:, 1].astype(jnp.int32)
    tile_expert, n_used, row_src, pos1, pos2 = _routing_plan(e1, e2, tm)
    ys = _experts(tile_expert, n_used, row_src, xt, moe_w_gate[0].astype(BF16),
                  moe_w_up[0].astype(BF16), moe_w_down[0].astype(BF16), tm=tm, nf=2)
    out = _combine(pos1, pos2, ys, route, h, row(final_norm), tm=tm)
    return out.reshape(bsz, seq, d)
```

```python
import functools

import jax
import jax.numpy as jnp
from jax import lax
from jax.experimental import pallas as pl
from jax.experimental.pallas import tpu as pltpu

CONV_WIDTH = 3
HEAD_DIM = 64
GROUP = 4
WINDOW = 128
BLOCK = 128
ROPE_THETA = 10000.0
N_EXPERTS = 8
RMS_EPS = 1e-5
NEG_BIG = -1e30

V7X_LANES = 128
V7X_SUBLANES = 8
VMEM_LIMIT = 56 * 1024 * 1024

BF16 = jnp.bfloat16
F32 = jnp.float32


def _dot(a, b):
    return jnp.dot(a, b, preferred_element_type=F32)


def _rmsnorm(x, g):
    ms = jnp.mean(x * x, axis=-1, keepdims=True)
    return x * lax.rsqrt(ms + RMS_EPS) * g


def _silu(g):
    return g * (1.0 / (1.0 + jnp.exp(-g)))


def _const_spec(shape):
    nd = len(shape)
    return pl.BlockSpec(shape, lambda *_: (0,) * nd, pipeline_mode=pl.Buffered(1))


def _params(sem):
    return pltpu.CompilerParams(dimension_semantics=sem, vmem_limit_bytes=VMEM_LIMIT)


def _conv_mixer_kernel(h_ref, carry0_ref, g_ref, win_ref, cw_ref, wout_ref, out_ref, vtail_ref,
                       carry_ref, *, tiles_per_seq):
    d = h_ref.shape[1]
    tm = h_ref.shape[0]

    @pl.when(pl.program_id(0) % tiles_per_seq == 0)
    def _():
        carry_ref[...] = carry0_ref[...]

    h = h_ref[...]
    xn = _rmsnorm(h, g_ref[...]).astype(BF16)
    b_gate = _dot(xn, win_ref[:, 0:d])
    c_gate = _dot(xn, win_ref[:, d:2 * d])
    u = _dot(xn, win_ref[:, 2 * d:3 * d])
    v = c_gate * u
    carry = carry_ref[...]
    row = lax.broadcasted_iota(jnp.int32, (V7X_SUBLANES, d), 0)
    r1 = pltpu.roll(v, 1, axis=0)
    r2 = pltpu.roll(v, 2, axis=0)
    head1 = jnp.where(row < 1, pltpu.roll(carry, 1, axis=0), r1[0:V7X_SUBLANES])
    head2 = jnp.where(row < 2, pltpu.roll(carry, 2, axis=0), r2[0:V7X_SUBLANES])
    v1 = jnp.concatenate([head1, r1[V7X_SUBLANES:]], axis=0)
    v2 = jnp.concatenate([head2, r2[V7X_SUBLANES:]], axis=0)
    conv = cw_ref[0:1, :] * v2 + cw_ref[1:2, :] * v1 + cw_ref[2:3, :] * v
    y = (b_gate * conv).astype(BF16)
    out_ref[...] = h + _dot(y, wout_ref[...])
    tail = v[tm - V7X_SUBLANES:tm]
    carry_ref[...] = tail
    vtail_ref[...] = tail


def _conv_mixer(h, carry0, g, w_in, conv_w, w_out, *, tm, tiles_per_seq):
    t, d = h.shape
    nt = t // tm
    return pl.pallas_call(
        functools.partial(_conv_mixer_kernel, tiles_per_seq=tiles_per_seq),
        grid=(nt,),
        in_specs=[
            pl.BlockSpec((tm, d), lambda i: (i, 0)),
            _const_spec((V7X_SUBLANES, d)),
            _const_spec((1, d)),
            _const_spec((d, 3 * d)),
            _const_spec((CONV_WIDTH, d)),
            _const_spec((d, d)),
        ],
        out_specs=[
            pl.BlockSpec((tm, d), lambda i: (i, 0)),
            pl.BlockSpec((V7X_SUBLANES, d), lambda i: (i, 0)),
        ],
        out_shape=[
            jax.ShapeDtypeStruct((t, d), F32),
            jax.ShapeDtypeStruct((nt * V7X_SUBLANES, d), F32),
        ],
        scratch_shapes=[pltpu.VMEM((V7X_SUBLANES, d), F32)],
        compiler_params=_params(("arbitrary",)),
        name="conv_mixer",
    )(h, carry0, g, w_in, conv_w, w_out)


def _ff_chunks(total, chunk):
    out, s = [], 0
    while s < total:
        out.append((s, min(chunk, total - s)))
        s += chunk
    return out


def _swiglu_partial(x, wg_ref, wu_ref, wd_ref, chunk):
    acc = None
    for s, n in _ff_chunks(wg_ref.shape[1], chunk):
        g = _dot(x, wg_ref[:, s:s + n])
        u = _dot(x, wu_ref[:, s:s + n])
        a = (_silu(g) * u).astype(BF16)
        part = _dot(a, wd_ref[s:s + n, :])
        acc = part if acc is None else acc + part
    return acc


def _ffn_kernel(h_ref, g_ref, wg_ref, wu_ref, wd_ref, out_ref, *, chunk):
    h = h_ref[...]
    xn = _rmsnorm(h, g_ref[...]).astype(BF16)
    out_ref[...] = h + _swiglu_partial(xn, wg_ref, wu_ref, wd_ref, chunk)


def _ffn(h, g, wg, wu, wd, *, tm, chunk=512):
    t, d = h.shape
    f = wg.shape[1]
    return pl.pallas_call(
        functools.partial(_ffn_kernel, chunk=chunk),
        grid=(t // tm,),
        in_specs=[
            pl.BlockSpec((tm, d), lambda i: (i, 0)),
            _const_spec((1, d)),
            _const_spec((d, f)),
            _const_spec((d, f)),
            _const_spec((f, d)),
        ],
        out_specs=pl.BlockSpec((tm, d), lambda i: (i, 0)),
        out_shape=jax.ShapeDtypeStruct((t, d), F32),
        compiler_params=_params(("arbitrary",)),
        name="dense_ffn",
    )(h, g, wg, wu, wd)


def _rope(x, cos, sin_signed, first_half):
    outs = []
    for j in range(x.shape[1] // V7X_LANES):
        xj = x[:, j * V7X_LANES:(j + 1) * V7X_LANES]
        partner = jnp.where(first_half, pltpu.roll(xj, V7X_LANES - HEAD_DIM // 2, axis=1),
                            pltpu.roll(xj, HEAD_DIM // 2, axis=1))
        outs.append(xj * cos + partner * sin_signed)
    return jnp.concatenate(outs, axis=1)


def _qkv_kernel(h_ref, g_ref, w_ref, b_ref, cos_ref, sin_ref, q_ref, k_ref, v_ref):
    d = h_ref.shape[1]
    xn = _rmsnorm(h_ref[...], g_ref[...]).astype(BF16)
    cos = cos_ref[...]
    sin_signed = sin_ref[...]
    lane = lax.broadcasted_iota(jnp.int32, cos.shape, 1)
    first_half = (lane % HEAD_DIM) < (HEAD_DIM // 2)
    q = _dot(xn, w_ref[:, 0:d]) + b_ref[:, 0:d]
    k = _dot(xn, w_ref[:, d:2 * d]) + b_ref[:, d:2 * d]
    v = _dot(xn, w_ref[:, 2 * d:3 * d]) + b_ref[:, 2 * d:3 * d]
    q_ref[...] = (_rope(q, cos, sin_signed, first_half) * (HEAD_DIM ** -0.5)).astype(BF16)
    k_ref[...] = _rope(k, cos, sin_signed, first_half).astype(BF16)
    v_ref[...] = v.astype(BF16)


def _qkv_rope(h, g, w_exp, b_exp, cos, sin_signed, *, tm, tiles_per_seq):
    t, d = h.shape
    tok = pl.BlockSpec((tm, d), lambda i: (i, 0))
    tab = pl.BlockSpec((tm, V7X_LANES), lambda i: (i % tiles_per_seq, 0))
    return pl.pallas_call(
        _qkv_kernel,
        grid=(t // tm,),
        in_specs=[tok, _const_spec((1, d)), _const_spec((d, 3 * d)), _const_spec((1, 3 * d)), tab, tab],
        out_specs=[tok, tok, tok],
        out_shape=[jax.ShapeDtypeStruct((t, d), BF16)] * 3,
        compiler_params=_params(("arbitrary",)),
        name="qkv_rope",
    )(h, g, w_exp, b_exp, cos, sin_signed)


def _attn_kernel(sinks_ref, q_ref, k_ref, v_ref, k0_ref, v0_ref, h_ref, wo_ref, out_ref,
                 kprev_ref, vprev_ref, o_ref, *, tiles_per_seq, first_block_min_key):
    tq, d = q_ref.shape
    nblk = tq // BLOCK
    n_kv = d // (GROUP * HEAD_DIM)
    hw = GROUP * HEAD_DIM
    first = (pl.program_id(0) % tiles_per_seq) == 0

    @pl.when(first)
    def _():
        kprev_ref[...] = k0_ref[...]
        vprev_ref[...] = v0_ref[...]

    jmin = jnp.where(first, first_block_min_key, 0)
    rows = GROUP * BLOCK
    r_idx = lax.broadcasted_iota(jnp.int32, (rows, 2 * BLOCK), 0) % BLOCK
    j_idx = lax.broadcasted_iota(jnp.int32, (rows, 2 * BLOCK), 1)
    band = (j_idx > r_idx) & (j_idx <= r_idx + WINDOW)
    band_first = band & (j_idx >= jmin)
    row_grp = lax.broadcasted_iota(jnp.int32, (rows, 1), 0) // BLOCK
    lane_grp = lax.broadcasted_iota(jnp.int32, (BLOCK, hw), 1) // HEAD_DIM
    grp_mask = [lane_grp == g for g in range(GROUP)]
    grp_mask_bf = [m.astype(BF16) for m in grp_mask]

    for b in range(nblk):
        cur = slice(b * BLOCK, (b + 1) * BLOCK)
        if b == 0:
            kp, vp, valid = kprev_ref[...], vprev_ref[...], band_first
        else:
            prev = slice((b - 1) * BLOCK, b * BLOCK)
            kp, vp, valid = k_ref[prev, :], v_ref[prev, :], band
        kw = jnp.concatenate([kp, k_ref[cur, :]], axis=0)
        vw = jnp.concatenate([vp, v_ref[cur, :]], axis=0)
        qb = q_ref[cur, :]
        outs = []
        for hh in range(n_kv):
            ls = slice(hh * hw, (hh + 1) * hw)
            q_h, kw_h, vw_h = qb[:, ls], kw[:, ls], vw[:, ls]
            qs = jnp.concatenate([q_h * grp_mask_bf[g] for g in range(GROUP)], axis=0)
            s = lax.dot_general(qs, kw_h, (((1,), (1,)), ((), ())), preferred_element_type=F32)
            s = jnp.where(valid, s, NEG_BIG)
            sink = jnp.zeros((rows, 1), F32)
            for g in range(GROUP):
                sink = jnp.where(row_grp == g, sinks_ref[hh * GROUP + g], sink)
            m = jnp.maximum(jnp.max(s, axis=-1, keepdims=True), sink)
            p = jnp.exp(s - m)
            denom = jnp.sum(p, axis=-1, keepdims=True) + jnp.exp(sink - m)
            inv = 1.0 / denom
            pb = p.astype(BF16)
            o_h = jnp.zeros((BLOCK, hw), F32)
            for g in range(GROUP):
                rs = slice(g * BLOCK, (g + 1) * BLOCK)
                o_h = jnp.where(grp_mask[g], _dot(pb[rs], vw_h) * inv[rs], o_h)
            outs.append(o_h)
        o_ref[cur, :] = jnp.concatenate(outs, axis=1).astype(BF16)

    last = slice((nblk - 1) * BLOCK, nblk * BLOCK)
    kprev_ref[...] = k_ref[last, :]
    vprev_ref[...] = v_ref[last, :]
    out_ref[...] = h_ref[...] + _dot(o_ref[...], wo_ref[...])


def _attention(sinks, q, k, v, k0, v0, h, w_o, *, tq, tiles_per_seq, first_block_min_key):
    t, d = h.shape
    tok = pl.BlockSpec((tq, d), lambda i: (i, 0))
    return pl.pallas_call(
        functools.partial(_attn_kernel, tiles_per_seq=tiles_per_seq,
                          first_block_min_key=first_block_min_key),
        grid=(t // tq,),
        in_specs=[
            pl.BlockSpec(memory_space=pltpu.SMEM),
            tok, tok, tok,
            _const_spec((BLOCK, d)), _const_spec((BLOCK, d)),
            tok,
            _const_spec((d, d)),
        ],
        out_specs=tok,
        out_shape=jax.ShapeDtypeStruct((t, d), F32),
        scratch_shapes=[pltpu.VMEM((BLOCK, d), BF16), pltpu.VMEM((BLOCK, d), BF16),
                        pltpu.VMEM((tq, d), BF16)],
        compiler_params=_params(("arbitrary",)),
        name="swa_attention",
    )(sinks, q, k, v, k0, v0, h, w_o)


def _to_row_tiles(dst_ref, x):
    tm = x.shape[0]
    for c in range(x.shape[1] // V7X_LANES):
        dst_ref[pl.ds(c, tm, stride=V7X_SUBLANES), :] = x[:, c * V7X_LANES:(c + 1) * V7X_LANES]


def _from_row_tiles(src_ref, tm):
    n = src_ref.shape[0] // tm
    return jnp.concatenate([src_ref[pl.ds(c, tm, stride=n), :] for c in range(n)], axis=1)


def _router_kernel(h_ref, g_ref, wr_ref, xt_ref, route_ref):
    xn = _rmsnorm(h_ref[...], g_ref[...])
    x_hi = xn.astype(BF16)
    x_lo = (xn - x_hi.astype(F32)).astype(BF16)
    w = wr_ref[...]
    w_hi = w.astype(BF16)
    w_lo = (w - w_hi.astype(F32)).astype(BF16)
    logits = _dot(x_hi, w_hi) + (_dot(x_hi, w_lo) + _dot(x_lo, w_hi))
    lane = lax.broadcasted_iota(jnp.int32, logits.shape, 1)
    neg_inf = jnp.float32(-jnp.inf)
    lg = jnp.where(lane < N_EXPERTS, logits, neg_inf)
    m1 = jnp.max(lg, axis=-1, keepdims=True)
    i1 = jnp.min(jnp.where(lg == m1, lane, V7X_LANES), axis=-1, keepdims=True)
    lg2 = jnp.where(lane == i1, neg_inf, lg)
    m2 = jnp.max(lg2, axis=-1, keepdims=True)
    i2 = jnp.min(jnp.where(lg2 == m2, lane, V7X_LANES), axis=-1, keepdims=True)
    e = jnp.exp(m2 - m1)
    g1 = 1.0 / (1.0 + e)
    g2 = e / (1.0 + e)
    route = jnp.where(lane == 0, i1.astype(F32),
                      jnp.where(lane == 1, i2.astype(F32),
                                jnp.where(lane == 2, g1, jnp.where(lane == 3, g2, 0.0))))
    route_ref[...] = route
    _to_row_tiles(xt_ref, xn)


def _router(h, g, w_router_padded, *, tm):
    t, d = h.shape
    rt = d // V7X_LANES
    return pl.pallas_call(
        _router_kernel,
        grid=(t // tm,),
        in_specs=[pl.BlockSpec((tm, d), lambda i: (i, 0)), _const_spec((1, d)),
                  _const_spec((d, V7X_LANES))],
        out_specs=[pl.BlockSpec((tm * rt, V7X_LANES), lambda i: (i, 0)),
                   pl.BlockSpec((tm, V7X_LANES), lambda i: (i, 0))],
        out_shape=[jax.ShapeDtypeStruct((t * rt, V7X_LANES), F32),
                   jax.ShapeDtypeStruct((t, V7X_LANES), F32)],
        compiler_params=_params(("arbitrary",)),
        name="moe_router",
    )(h, g, w_router_padded)


def _row_copy(src_ref, src_row, dst_ref, dst_row, sem, rt):
    return pltpu.make_async_copy(src_ref.at[pl.ds(pl.multiple_of(src_row * rt, rt), rt), :],
                                 dst_ref.at[pl.ds(pl.multiple_of(dst_row * rt, rt), rt), :], sem)


def _wait_rows(hbm_ref, vmem_ref, sem):
    pltpu.make_async_copy(hbm_ref.at[pl.ds(0, vmem_ref.shape[0]), :], vmem_ref, sem).wait()


def _expert_kernel(w_tile_ref, w_expert_ref, w_first_ref, w_last_ref, w_lo_ref, w_hi_ref, n_work_ref,
                   cur_ref, nxt_ref, xt_hbm, wg_ref, wu_ref, wd_ref, y_hbm,
                   gbuf_ref, xb_ref, acc_ref, stage_ref, gsem, ssem, flag_ref, *, chunk):
    del w_expert_ref
    tm, d = xb_ref.shape
    rt = d // V7X_LANES
    n_tiles = y_hbm.shape[0] // (tm * rt)
    w = pl.program_id(0)
    n_work = n_work_ref[0]
    active = w < n_work
    first = active & (w_first_ref[w] == 1)
    last = active & (w_last_ref[w] == 1)

    def issue_gather(idx_ref):
        def body(r, c):
            _row_copy(xt_hbm, idx_ref[r] // 2, gbuf_ref, r, gsem, rt).start()
            return c
        lax.fori_loop(0, tm, body, 0)

    @pl.when(w == 0)
    def _():
        flag_ref[0] = 0
        issue_gather(cur_ref)

    @pl.when(first)
    def _():
        _wait_rows(xt_hbm, gbuf_ref, gsem)
        xb_ref[...] = _from_row_tiles(gbuf_ref, tm).astype(BF16)

        @pl.when(w_tile_ref[w] + 1 < n_tiles)
        def _():
            issue_gather(nxt_ref)

    @pl.when(active)
    def _():
        part = _swiglu_partial(xb_ref[...], wg_ref, wu_ref, wd_ref, chunk)
        r = lax.broadcasted_iota(jnp.int32, (tm, 1), 0)
        part = jnp.where((r >= w_lo_ref[w]) & (r < w_hi_ref[w]), part, 0.0)

        @pl.when(first)
        def _():
            acc_ref[...] = part

        @pl.when(jnp.logical_not(first))
        def _():
            acc_ref[...] += part

    @pl.when(last)
    def _():
        @pl.when(flag_ref[0] == 1)
        def _():
            _wait_rows(y_hbm, stage_ref, ssem)

        _to_row_tiles(stage_ref, acc_ref[...])

        def body(r, c):
            _row_copy(stage_ref, r, y_hbm, cur_ref[r], ssem, rt).start()
            return c
        lax.fori_loop(0, tm, body, 0)
        flag_ref[0] = 1

    @pl.when((w == pl.num_programs(0) - 1) & (flag_ref[0] == 1))
    def _():
        _wait_rows(y_hbm, stage_ref, ssem)


def _experts(plan, sorted_a, xt, wg, wu, wd, *, tm, chunk=512):
    n_e, d, f = wg.shape
    rt = d // V7X_LANES
    n_assign = sorted_a.shape[0]
    n_tiles = n_assign // tm
    n_items = n_tiles + n_e - 1
    last_tile = n_tiles - 1
    grid_spec = pltpu.PrefetchScalarGridSpec(
        num_scalar_prefetch=7,
        grid=(n_items,),
        in_specs=[
            pl.BlockSpec((tm,), lambda w, wt, *_: (wt[w],), memory_space=pltpu.SMEM),
            pl.BlockSpec((tm,), lambda w, wt, *_: (jnp.minimum(wt[w] + 1, last_tile),),
                         memory_space=pltpu.SMEM),
            pl.BlockSpec(memory_space=pl.ANY),
            pl.BlockSpec((None, d, f), lambda w, wt, we, *_: (we[w], 0, 0)),
            pl.BlockSpec((None, d, f), lambda w, wt, we, *_: (we[w], 0, 0)),
            pl.BlockSpec((None, f, d), lambda w, wt, we, *_: (we[w], 0, 0)),
        ],
        out_specs=pl.BlockSpec(memory_space=pl.ANY),
        scratch_shapes=[pltpu.VMEM((tm * rt, V7X_LANES), F32), pltpu.VMEM((tm, d), BF16),
                        pltpu.VMEM((tm, d), F32), pltpu.VMEM((tm * rt, V7X_LANES), F32),
                        pltpu.SemaphoreType.DMA, pltpu.SemaphoreType.DMA, pltpu.SMEM((1,), jnp.int32)],
    )
    return pl.pallas_call(
        functools.partial(_expert_kernel, chunk=chunk),
        grid_spec=grid_spec,
        out_shape=jax.ShapeDtypeStruct((n_assign * rt, V7X_LANES), F32),
        compiler_params=_params(("arbitrary",)),
        name="moe_experts",
    )(*plan, sorted_a, sorted_a, xt, wg, wu, wd)


def _combine_kernel(y_ref, route_ref, h_ref, g_ref, out_ref):
    tm, d = h_ref.shape
    rt = d // V7X_LANES
    route = route_ref[...]
    g1, g2 = route[:, 2:3], route[:, 3:4]
    y1 = jnp.concatenate([y_ref[pl.ds(c, tm, stride=2 * rt), :] for c in range(rt)], axis=1)
    y2 = jnp.concatenate([y_ref[pl.ds(rt + c, tm, stride=2 * rt), :] for c in range(rt)], axis=1)
    out_ref[...] = _rmsnorm(h_ref[...] + g1 * y1 + g2 * y2, g_ref[...])


def _combine(y, route, h, g, *, tm):
    t, d = h.shape
    rt = d // V7X_LANES
    return pl.pallas_call(
        _combine_kernel,
        grid=(t // tm,),
        in_specs=[pl.BlockSpec((tm * 2 * rt, V7X_LANES), lambda i: (i, 0)),
                  pl.BlockSpec((tm, V7X_LANES), lambda i: (i, 0)),
                  pl.BlockSpec((tm, d), lambda i: (i, 0)), _const_spec((1, d))],
        out_specs=pl.BlockSpec((tm, d), lambda i: (i, 0)),
        out_shape=jax.ShapeDtypeStruct((t, d), F32),
        compiler_params=_params(("arbitrary",)),
        name="moe_combine_norm",
    )(y, route, h, g)


def _routing_plan(e1, e2, tm):
    flat_e = jnp.stack([e1, e2], axis=1).reshape(-1)
    n_assign = flat_e.shape[0]
    ids = jnp.arange(n_assign, dtype=jnp.int32)
    _, sorted_a = lax.sort((flat_e, ids), num_keys=1, is_stable=True)
    experts = jnp.arange(N_EXPERTS, dtype=jnp.int32)
    counts = jnp.sum((flat_e[:, None] == experts[None, :]).astype(jnp.int32), axis=0)
    seg_end = jnp.cumsum(counts)
    seg_start = seg_end - counts
    first_tile = seg_start // tm
    n_items = jnp.where(counts > 0, (seg_end - 1) // tm - first_tile + 1, 0)
    item_end = jnp.cumsum(n_items)
    item_start = item_end - n_items
    n_work = item_end[-1]
    max_items = n_assign // tm + N_EXPERTS - 1
    w = jnp.minimum(jnp.arange(max_items, dtype=jnp.int32), n_work - 1)
    w_expert = jnp.sum((w[:, None] >= item_end[None, :]).astype(jnp.int32), axis=1)
    pick = lambda v: jnp.sum(jnp.where(w_expert[:, None] == experts[None, :], v[None, :], 0), axis=1)
    w_tile = pick(first_tile) + (w - pick(item_start))
    w_lo = jnp.maximum(pick(seg_start) - w_tile * tm, 0)
    w_hi = jnp.minimum(pick(seg_end) - w_tile * tm, tm)
    prev_tile = jnp.concatenate([jnp.full((1,), -1, jnp.int32), w_tile[:-1]])
    next_tile = jnp.concatenate([w_tile[1:], jnp.full((1,), -1, jnp.int32)])
    is_last_item = jnp.arange(max_items) == n_work - 1
    w_first = (w_tile != prev_tile).astype(jnp.int32)
    w_last = ((w_tile != next_tile) | is_last_item).astype(jnp.int32)
    i32 = lambda v: v.astype(jnp.int32)
    plan = (i32(w_tile), i32(w_expert), w_first, w_last, i32(w_lo), i32(w_hi), i32(n_work).reshape(1))
    return plan, sorted_a


def _rope_tables(length):
    pos = jnp.arange(length, dtype=F32)
    inv = ROPE_THETA ** (-jnp.arange(0, HEAD_DIM, 2, dtype=F32) / HEAD_DIM)
    ang = pos[:, None] * inv[None, :]
    cos, sin = jnp.cos(ang), jnp.sin(ang)
    reps = V7X_LANES // HEAD_DIM
    cos_l = jnp.tile(jnp.concatenate([cos, cos], axis=1), (1, reps))
    sin_l = jnp.tile(jnp.concatenate([-sin, sin], axis=1), (1, reps))
    return cos_l, sin_l


def _expand_kv_cols(w, n_kv):
    lead = w.shape[:-1]
    w = w.reshape(lead + (n_kv, 1, HEAD_DIM))
    w = jnp.broadcast_to(w, lead + (n_kv, GROUP, HEAD_DIM))
    return w.reshape(lead + (n_kv * GROUP * HEAD_DIM,))


def kernel(x, meta_tokens, conv_norm, conv_w_in, conv_w, conv_w_out, ffn_norm, ffn_w_gate, ffn_w_up, ffn_w_down, attn_norm, attn_w_qkv, attn_b_qkv, attn_sinks, attn_w_o, moe_norm, moe_w_router, moe_w_gate, moe_w_up, moe_w_down, final_norm):
    bsz, seq, d = x.shape
    n_meta = meta_tokens.shape[0]
    depth = conv_norm.shape[0] + attn_norm.shape[0]
    assert depth == 2 and conv_norm.shape[0] == 1 and attn_norm.shape[0] == 1
    assert n_meta % V7X_SUBLANES == 0 and n_meta <= BLOCK and n_meta >= CONV_WIDTH - 1
    q_dim = d
    kv_dim = (attn_w_qkv.shape[2] - q_dim) // 2
    n_kv = kv_dim // HEAD_DIM
    assert q_dim == n_kv * GROUP * HEAD_DIM

    tm = 512
    assert seq % tm == 0 and tm % BLOCK == 0
    tiles_per_seq = seq // tm
    t = bsz * seq
    row = lambda a: a.reshape(1, -1).astype(F32)

    h = x.reshape(t, d)
    hm = meta_tokens.astype(F32)

    w_in = conv_w_in[0].astype(BF16)
    w_out = conv_w_out[0].astype(BF16)
    zero_carry = jnp.zeros((V7X_SUBLANES, d), F32)
    hm, vtail_m = _conv_mixer(hm, zero_carry, row(conv_norm[0]), w_in, conv_w[0], w_out,
                              tm=n_meta, tiles_per_seq=1)
    h, _ = _conv_mixer(h, vtail_m, row(conv_norm[0]), w_in, conv_w[0], w_out,
                       tm=tm, tiles_per_seq=tiles_per_seq)
    wg, wu, wd = ffn_w_gate[0].astype(BF16), ffn_w_up[0].astype(BF16), ffn_w_down[0].astype(BF16)
    hm = _ffn(hm, row(ffn_norm[0]), wg, wu, wd, tm=n_meta)
    h = _ffn(h, row(ffn_norm[0]), wg, wu, wd, tm=tm)

    w_qkv, b_qkv = attn_w_qkv[0], attn_b_qkv[0]
    w_exp = jnp.concatenate([w_qkv[:, :q_dim],
                             _expand_kv_cols(w_qkv[:, q_dim:q_dim + kv_dim], n_kv),
                             _expand_kv_cols(w_qkv[:, q_dim + kv_dim:], n_kv)], axis=1).astype(BF16)
    b_exp = jnp.concatenate([b_qkv[:q_dim],
                             _expand_kv_cols(b_qkv[q_dim:q_dim + kv_dim], n_kv),
                             _expand_kv_cols(b_qkv[q_dim + kv_dim:], n_kv)]).reshape(1, -1).astype(F32)
    cos_l, sin_l = _rope_tables(n_meta + seq)
    _, km, vm = _qkv_rope(hm, row(attn_norm[0]), w_exp, b_exp, cos_l[:n_meta], sin_l[:n_meta],
                          tm=n_meta, tiles_per_seq=1)
    q, k, v = _qkv_rope(h, row(attn_norm[0]), w_exp, b_exp, cos_l[n_meta:], sin_l[n_meta:],
                        tm=tm, tiles_per_seq=tiles_per_seq)
    k0 = jnp.zeros((BLOCK, d), BF16).at[BLOCK - n_meta:].set(km)
    v0 = jnp.zeros((BLOCK, d), BF16).at[BLOCK - n_meta:].set(vm)
    h = _attention(attn_sinks[0].astype(F32), q, k, v, k0, v0, h, attn_w_o[0].astype(BF16),
                   tq=tm, tiles_per_seq=tiles_per_seq, first_block_min_key=BLOCK - n_meta)

    w_r = jnp.zeros((d, V7X_LANES), F32).at[:, :N_EXPERTS].set(moe_w_router[0].astype(F32))
    xt, route = _router(h, row(moe_norm[0]), w_r, tm=tm)
    e1 = route[:, 0].astype(jnp.int32)
    e2 = route[:, 1].astype(jnp.int32)
    plan, sorted_a = _routing_plan(e1, e2, tm)
    y = _experts(plan, sorted_a, xt, moe_w_gate[0].astype(BF16), moe_w_up[0].astype(BF16),
                 moe_w_down[0].astype(BF16), tm=tm)
    out = _combine(y, route, h, row(final_norm), tm=tm)
    return out.reshape(bsz, seq, d)
```

```python
import functools

import jax
import jax.numpy as jnp
from jax import lax
from jax.experimental import pallas as pl
from jax.experimental.pallas import tpu as pltpu

CONV_WIDTH = 3
HEAD_DIM = 64
GROUP = 4
WINDOW = 128
BLOCK = 128
ROPE_THETA = 10000.0
N_EXPERTS = 8
RMS_EPS = 1e-5
NEG_BIG = -1e30

V7X_LANES = 128
V7X_SUBLANES = 8
VMEM_LIMIT = 56 * 1024 * 1024

BF16 = jnp.bfloat16
F32 = jnp.float32


def _dot(a, b):
    return jnp.dot(a, b, preferred_element_type=F32)


def _rmsnorm(x, g):
    ms = jnp.mean(x * x, axis=-1, keepdims=True)
    return x * lax.rsqrt(ms + RMS_EPS) * g


def _silu(g):
    return g * (1.0 / (1.0 + jnp.exp(-g)))


def _const_spec(shape):
    nd = len(shape)
    return pl.BlockSpec(shape, lambda *_: (0,) * nd, pipeline_mode=pl.Buffered(1))


def _params(sem):
    return pltpu.CompilerParams(dimension_semantics=sem, vmem_limit_bytes=VMEM_LIMIT)


def _conv_mixer_kernel(h_ref, carry0_ref, g_ref, win_ref, cw_ref, wout_ref, out_ref, vtail_ref,
                       carry_ref, *, tiles_per_seq):
    d = h_ref.shape[1]
    tm = h_ref.shape[0]

    @pl.when(pl.program_id(0) % tiles_per_seq == 0)
    def _():
        carry_ref[...] = carry0_ref[...]

    h = h_ref[...]
    xn = _rmsnorm(h, g_ref[...]).astype(BF16)
    b_gate = _dot(xn, win_ref[:, 0:d])
    c_gate = _dot(xn, win_ref[:, d:2 * d])
    u = _dot(xn, win_ref[:, 2 * d:3 * d])
    v = c_gate * u
    carry = carry_ref[...]
    row = lax.broadcasted_iota(jnp.int32, (V7X_SUBLANES, d), 0)
    r1 = pltpu.roll(v, 1, axis=0)
    r2 = pltpu.roll(v, 2, axis=0)
    head1 = jnp.where(row < 1, pltpu.roll(carry, 1, axis=0), r1[0:V7X_SUBLANES])
    head2 = jnp.where(row < 2, pltpu.roll(carry, 2, axis=0), r2[0:V7X_SUBLANES])
    v1 = jnp.concatenate([head1, r1[V7X_SUBLANES:]], axis=0)
    v2 = jnp.concatenate([head2, r2[V7X_SUBLANES:]], axis=0)
    conv = cw_ref[0:1, :] * v2 + cw_ref[1:2, :] * v1 + cw_ref[2:3, :] * v
    y = (b_gate * conv).astype(BF16)
    out_ref[...] = h + _dot(y, wout_ref[...])
    tail = v[tm - V7X_SUBLANES:tm]
    carry_ref[...] = tail
    vtail_ref[...] = tail


def _conv_mixer(h, carry0, g, w_in, conv_w, w_out, *, tm, tiles_per_seq):
    t, d = h.shape
    nt = t // tm
    return pl.pallas_call(
        functools.partial(_conv_mixer_kernel, tiles_per_seq=tiles_per_seq),
        grid=(nt,),
        in_specs=[
            pl.BlockSpec((tm, d), lambda i: (i, 0)),
            _const_spec((V7X_SUBLANES, d)),
            _const_spec((1, d)),
            _const_spec((d, 3 * d)),
            _const_spec((CONV_WIDTH, d)),
            _const_spec((d, d)),
        ],
        out_specs=[
            pl.BlockSpec((tm, d), lambda i: (i, 0)),
            pl.BlockSpec((V7X_SUBLANES, d), lambda i: (i, 0)),
        ],
        out_shape=[
            jax.ShapeDtypeStruct((t, d), F32),
            jax.ShapeDtypeStruct((nt * V7X_SUBLANES, d), F32),
        ],
        scratch_shapes=[pltpu.VMEM((V7X_SUBLANES, d), F32)],
        compiler_params=_params(("arbitrary",)),
        name="conv_mixer",
    )(h, carry0, g, w_in, conv_w, w_out)


def _ff_chunks(total, chunk):
    out, s = [], 0
    while s < total:
        out.append((s, min(chunk, total - s)))
        s += chunk
    return out


def _swiglu_partial(x, wg_ref, wu_ref, wd_ref, chunk, after_chunk=None):
    acc = None
    pieces = _ff_chunks(wg_ref.shape[1], chunk)
    for i, (s, n) in enumerate(pieces):
        g = _dot(x, wg_ref[:, s:s + n])
        u = _dot(x, wu_ref[:, s:s + n])
        a = (_silu(g) * u).astype(BF16)
        part = _dot(a, wd_ref[s:s + n, :])
        acc = part if acc is None else acc + part
        if after_chunk is not None:
            after_chunk(i, len(pieces))
    return acc


def _ffn_kernel(h_ref, g_ref, wg_ref, wu_ref, wd_ref, out_ref, *, chunk):
    h = h_ref[...]
    xn = _rmsnorm(h, g_ref[...]).astype(BF16)
    out_ref[...] = h + _swiglu_partial(xn, wg_ref, wu_ref, wd_ref, chunk)


def _ffn(h, g, wg, wu, wd, *, tm, chunk=512):
    t, d = h.shape
    f = wg.shape[1]
    return pl.pallas_call(
        functools.partial(_ffn_kernel, chunk=chunk),
        grid=(t // tm,),
        in_specs=[
            pl.BlockSpec((tm, d), lambda i: (i, 0)),
            _const_spec((1, d)),
            _const_spec((d, f)),
            _const_spec((d, f)),
            _const_spec((f, d)),
        ],
        out_specs=pl.BlockSpec((tm, d), lambda i: (i, 0)),
        out_shape=jax.ShapeDtypeStruct((t, d), F32),
        compiler_params=_params(("arbitrary",)),
        name="dense_ffn",
    )(h, g, wg, wu, wd)


def _rope(x, cos, sin_signed, first_half):
    outs = []
    for j in range(x.shape[1] // V7X_LANES):
        xj = x[:, j * V7X_LANES:(j + 1) * V7X_LANES]
        partner = jnp.where(first_half, pltpu.roll(xj, V7X_LANES - HEAD_DIM // 2, axis=1),
                            pltpu.roll(xj, HEAD_DIM // 2, axis=1))
        outs.append(xj * cos + partner * sin_signed)
    return jnp.concatenate(outs, axis=1)


def _qkv_kernel(h_ref, g_ref, w_ref, b_ref, cos_ref, sin_ref, q_ref, k_ref, v_ref):
    d = h_ref.shape[1]
    xn = _rmsnorm(h_ref[...], g_ref[...]).astype(BF16)
    cos = cos_ref[...]
    sin_signed = sin_ref[...]
    lane = lax.broadcasted_iota(jnp.int32, cos.shape, 1)
    first_half = (lane % HEAD_DIM) < (HEAD_DIM // 2)
    q = _dot(xn, w_ref[:, 0:d]) + b_ref[:, 0:d]
    k = _dot(xn, w_ref[:, d:2 * d]) + b_ref[:, d:2 * d]
    v = _dot(xn, w_ref[:, 2 * d:3 * d]) + b_ref[:, 2 * d:3 * d]
    q_ref[...] = (_rope(q, cos, sin_signed, first_half) * (HEAD_DIM ** -0.5)).astype(BF16)
    k_ref[...] = _rope(k, cos, sin_signed, first_half).astype(BF16)
    v_ref[...] = v.astype(BF16)


def _qkv_rope(h, g, w_exp, b_exp, cos, sin_signed, *, tm, tiles_per_seq):
    t, d = h.shape
    tok = pl.BlockSpec((tm, d), lambda i: (i, 0))
    tab = pl.BlockSpec((tm, V7X_LANES), lambda i: (i % tiles_per_seq, 0))
    return pl.pallas_call(
        _qkv_kernel,
        grid=(t // tm,),
        in_specs=[tok, _const_spec((1, d)), _const_spec((d, 3 * d)), _const_spec((1, 3 * d)), tab, tab],
        out_specs=[tok, tok, tok],
        out_shape=[jax.ShapeDtypeStruct((t, d), BF16)] * 3,
        compiler_params=_params(("arbitrary",)),
        name="qkv_rope",
    )(h, g, w_exp, b_exp, cos, sin_signed)


def _attn_kernel(sinks_ref, q_ref, k_ref, v_ref, k0_ref, v0_ref, h_ref, wo_ref, out_ref,
                 kprev_ref, vprev_ref, o_ref, *, tiles_per_seq, first_block_min_key):
    tq, d = q_ref.shape
    nblk = tq // BLOCK
    n_kv = d // (GROUP * HEAD_DIM)
    hw = GROUP * HEAD_DIM
    first = (pl.program_id(0) % tiles_per_seq) == 0

    @pl.when(first)
    def _():
        kprev_ref[...] = k0_ref[...]
        vprev_ref[...] = v0_ref[...]

    jmin = jnp.where(first, first_block_min_key, 0)
    rows = GROUP * BLOCK
    r_idx = lax.broadcasted_iota(jnp.int32, (rows, 2 * BLOCK), 0) % BLOCK
    j_idx = lax.broadcasted_iota(jnp.int32, (rows, 2 * BLOCK), 1)
    band = (j_idx > r_idx) & (j_idx <= r_idx + WINDOW)
    band_first = band & (j_idx >= jmin)
    row_grp = lax.broadcasted_iota(jnp.int32, (rows, 1), 0) // BLOCK
    lane_grp = lax.broadcasted_iota(jnp.int32, (BLOCK, hw), 1) // HEAD_DIM
    grp_mask = [lane_grp == g for g in range(GROUP)]
    grp_mask_bf = [m.astype(BF16) for m in grp_mask]

    for b in range(nblk):
        cur = slice(b * BLOCK, (b + 1) * BLOCK)
        if b == 0:
            kp, vp, valid = kprev_ref[...], vprev_ref[...], band_first
        else:
            prev = slice((b - 1) * BLOCK, b * BLOCK)
            kp, vp, valid = k_ref[prev, :], v_ref[prev, :], band
        kw = jnp.concatenate([kp, k_ref[cur, :]], axis=0)
        vw = jnp.concatenate([vp, v_ref[cur, :]], axis=0)
        qb = q_ref[cur, :]
        outs = []
        for hh in range(n_kv):
            ls = slice(hh * hw, (hh + 1) * hw)
            q_h, kw_h, vw_h = qb[:, ls], kw[:, ls], vw[:, ls]
            qs = jnp.concatenate([q_h * grp_mask_bf[g] for g in range(GROUP)], axis=0)
            s = lax.dot_general(qs, kw_h, (((1,), (1,)), ((), ())), preferred_element_type=F32)
            s = jnp.where(valid, s, NEG_BIG)
            sink = jnp.zeros((rows, 1), F32)
            for g in range(GROUP):
                sink = jnp.where(row_grp == g, sinks_ref[hh * GROUP + g], sink)
            m = jnp.maximum(jnp.max(s, axis=-1, keepdims=True), sink)
            p = jnp.exp(s - m)
            denom = jnp.sum(p, axis=-1, keepdims=True) + jnp.exp(sink - m)
            inv = 1.0 / denom
            pb = p.astype(BF16)
            o_h = jnp.zeros((BLOCK, hw), F32)
            for g in range(GROUP):
                rs = slice(g * BLOCK, (g + 1) * BLOCK)
                o_h = jnp.where(grp_mask[g], _dot(pb[rs], vw_h) * inv[rs], o_h)
            outs.append(o_h)
        o_ref[cur, :] = jnp.concatenate(outs, axis=1).astype(BF16)

    last = slice((nblk - 1) * BLOCK, nblk * BLOCK)
    kprev_ref[...] = k_ref[last, :]
    vprev_ref[...] = v_ref[last, :]
    out_ref[...] = h_ref[...] + _dot(o_ref[...], wo_ref[...])


def _attention(sinks, q, k, v, k0, v0, h, w_o, *, tq, tiles_per_seq, first_block_min_key):
    t, d = h.shape
    tok = pl.BlockSpec((tq, d), lambda i: (i, 0))
    return pl.pallas_call(
        functools.partial(_attn_kernel, tiles_per_seq=tiles_per_seq,
                          first_block_min_key=first_block_min_key),
        grid=(t // tq,),
        in_specs=[
            pl.BlockSpec(memory_space=pltpu.SMEM),
            tok, tok, tok,
            _const_spec((BLOCK, d)), _const_spec((BLOCK, d)),
            tok,
            _const_spec((d, d)),
        ],
        out_specs=tok,
        out_shape=jax.ShapeDtypeStruct((t, d), F32),
        scratch_shapes=[pltpu.VMEM((BLOCK, d), BF16), pltpu.VMEM((BLOCK, d), BF16),
                        pltpu.VMEM((tq, d), BF16)],
        compiler_params=_params(("arbitrary",)),
        name="swa_attention",
    )(sinks, q, k, v, k0, v0, h, w_o)


def _to_row_tiles(dst_ref, x):
    tm = x.shape[0]
    for c in range(x.shape[1] // V7X_LANES):
        dst_ref[pl.ds(c, tm, stride=V7X_SUBLANES), :] = x[:, c * V7X_LANES:(c + 1) * V7X_LANES]


def _from_row_tiles(src_ref, tm):
    n = src_ref.shape[0] // tm
    return jnp.concatenate([src_ref[pl.ds(c, tm, stride=n), :] for c in range(n)], axis=1)


def _router_kernel(h_ref, g_ref, wr_ref, xt_ref, route_ref):
    xn = _rmsnorm(h_ref[...], g_ref[...])
    x_hi = xn.astype(BF16)
    x_lo = (xn - x_hi.astype(F32)).astype(BF16)
    w = wr_ref[...]
    w_hi = w.astype(BF16)
    w_lo = (w - w_hi.astype(F32)).astype(BF16)
    logits = _dot(x_hi, w_hi) + (_dot(x_hi, w_lo) + _dot(x_lo, w_hi))
    lane = lax.broadcasted_iota(jnp.int32, logits.shape, 1)
    neg_inf = jnp.float32(-jnp.inf)
    lg = jnp.where(lane < N_EXPERTS, logits, neg_inf)
    m1 = jnp.max(lg, axis=-1, keepdims=True)
    i1 = jnp.min(jnp.where(lg == m1, lane, V7X_LANES), axis=-1, keepdims=True)
    lg2 = jnp.where(lane == i1, neg_inf, lg)
    m2 = jnp.max(lg2, axis=-1, keepdims=True)
    i2 = jnp.min(jnp.where(lg2 == m2, lane, V7X_LANES), axis=-1, keepdims=True)
    e = jnp.exp(m2 - m1)
    g1 = 1.0 / (1.0 + e)
    g2 = e / (1.0 + e)
    route = jnp.where(lane == 0, i1.astype(F32),
                      jnp.where(lane == 1, i2.astype(F32),
                                jnp.where(lane == 2, g1, jnp.where(lane == 3, g2, 0.0))))
    route_ref[...] = route
    _to_row_tiles(xt_ref, xn)


def _router(h, g, w_router_padded, *, tm):
    t, d = h.shape
    rt = d // V7X_LANES
    return pl.pallas_call(
        _router_kernel,
        grid=(t // tm,),
        in_specs=[pl.BlockSpec((tm, d), lambda i: (i, 0)), _const_spec((1, d)),
                  _const_spec((d, V7X_LANES))],
        out_specs=[pl.BlockSpec((tm * rt, V7X_LANES), lambda i: (i, 0)),
                   pl.BlockSpec((tm, V7X_LANES), lambda i: (i, 0))],
        out_shape=[jax.ShapeDtypeStruct((t * rt, V7X_LANES), F32),
                   jax.ShapeDtypeStruct((t, V7X_LANES), F32)],
        compiler_params=_params(("arbitrary",)),
        name="moe_router",
    )(h, g, w_router_padded)


def _row_copy(src_ref, src_row, dst_ref, dst_row, sem, rt):
    return pltpu.make_async_copy(src_ref.at[pl.ds(pl.multiple_of(src_row * rt, rt), rt), :],
                                 dst_ref.at[pl.ds(pl.multiple_of(dst_row * rt, rt), rt), :], sem)


def _wait_rows(hbm_ref, vmem_ref, sem):
    pltpu.make_async_copy(hbm_ref.at[pl.ds(0, vmem_ref.shape[0]), :], vmem_ref, sem).wait()


def _expert_kernel(w_tile_ref, w_expert_ref, w_first_ref, w_last_ref, w_lo_ref, w_hi_ref, n_work_ref,
                   src0_ref, src_next_ref, dst_prev_ref, dst_cur_ref, xt_hbm, wg_ref, wu_ref, wd_ref, y_hbm,
                   gbuf_ref, xb_ref, acc_ref, stage_ref, gsem, ssem, *, chunk):
    del w_tile_ref, w_expert_ref
    tm, d = xb_ref.shape
    rt = d // V7X_LANES
    w = pl.program_id(0)
    n_work = n_work_ref[0]
    active = w < n_work
    first = active & (w_first_ref[w] == 1)
    last = active & (w_last_ref[w] == 1)
    row_id = lax.broadcasted_iota(jnp.int32, (tm, 1), 0)

    def masked(part):
        return jnp.where((row_id >= w_lo_ref[w]) & (row_id < w_hi_ref[w]), part, 0.0)

    @pl.when(w == 0)
    def _():
        stage_ref[...] = jnp.zeros_like(stage_ref)

        def body(r, c):
            _row_copy(xt_hbm, src0_ref[r], gbuf_ref, r, gsem, rt).start()
            return c
        lax.fori_loop(0, tm, body, 0)

    @pl.when(first)
    def _():
        _wait_rows(xt_hbm, gbuf_ref, gsem)
        xb_ref[...] = _from_row_tiles(gbuf_ref, tm).astype(BF16)

        def side_dmas(i, n):
            n_sc = max(1, n // 2)
            if i < n_sc:
                for r in range(i * tm // n_sc, (i + 1) * tm // n_sc):
                    _row_copy(stage_ref, r, y_hbm, dst_prev_ref[r], ssem, rt).start()
            else:
                j, m = i - n_sc, n - n_sc
                for r in range(j * tm // m, (j + 1) * tm // m):
                    _row_copy(xt_hbm, src_next_ref[r], gbuf_ref, r, gsem, rt).start()

        acc_ref[...] = masked(_swiglu_partial(xb_ref[...], wg_ref, wu_ref, wd_ref, chunk, side_dmas))

    @pl.when(active & jnp.logical_not(first))
    def _():
        acc_ref[...] += masked(_swiglu_partial(xb_ref[...], wg_ref, wu_ref, wd_ref, chunk))

    @pl.when(last)
    def _():
        _wait_rows(y_hbm, stage_ref, ssem)
        _to_row_tiles(stage_ref, acc_ref[...])

    @pl.when(w == n_work - 1)
    def _():
        def body(r, c):
            _row_copy(stage_ref, r, y_hbm, dst_cur_ref[r], ssem, rt).start()
            return c
        lax.fori_loop(0, tm, body, 0)
        _wait_rows(y_hbm, stage_ref, ssem)
        _wait_rows(xt_hbm, gbuf_ref, gsem)


def _experts(plan, sorted_a, xt, wg, wu, wd, *, tm, chunk=512):
    n_e, d, f = wg.shape
    rt = d // V7X_LANES
    n_assign = sorted_a.shape[0]
    n_tiles = n_assign // tm
    n_items = n_tiles + n_e - 1
    dst_rows = jnp.concatenate([sorted_a, n_assign + jnp.arange(tm, dtype=jnp.int32)])
    src_rows = jnp.concatenate([sorted_a // 2, jnp.zeros((tm,), jnp.int32)])
    smem_tile = lambda fn: pl.BlockSpec((tm,), fn, memory_space=pltpu.SMEM)
    grid_spec = pltpu.PrefetchScalarGridSpec(
        num_scalar_prefetch=7,
        grid=(n_items,),
        in_specs=[
            smem_tile(lambda w, wt, *_: (0,)),
            smem_tile(lambda w, wt, *_: (wt[w] + 1,)),
            smem_tile(lambda w, wt, *_: (jnp.where(wt[w] == 0, n_tiles, wt[w] - 1),)),
            smem_tile(lambda w, wt, *_: (wt[w],)),
            pl.BlockSpec(memory_space=pl.ANY),
            pl.BlockSpec((None, d, f), lambda w, wt, we, *_: (we[w], 0, 0)),
            pl.BlockSpec((None, d, f), lambda w, wt, we, *_: (we[w], 0, 0)),
            pl.BlockSpec((None, f, d), lambda w, wt, we, *_: (we[w], 0, 0)),
        ],
        out_specs=pl.BlockSpec(memory_space=pl.ANY),
        scratch_shapes=[pltpu.VMEM((tm * rt, V7X_LANES), F32), pltpu.VMEM((tm, d), BF16),
                        pltpu.VMEM((tm, d), F32), pltpu.VMEM((tm * rt, V7X_LANES), F32),
                        pltpu.SemaphoreType.DMA, pltpu.SemaphoreType.DMA],
    )
    return pl.pallas_call(
        functools.partial(_expert_kernel, chunk=chunk),
        grid_spec=grid_spec,
        out_shape=jax.ShapeDtypeStruct(((n_assign + tm) * rt, V7X_LANES), F32),
        compiler_params=_params(("arbitrary",)),
        name="moe_experts",
    )(*plan, src_rows, src_rows, dst_rows, dst_rows, xt, wg, wu, wd)


def _combine_kernel(y_ref, route_ref, h_ref, g_ref, out_ref):
    tm, d = h_ref.shape
    rt = d // V7X_LANES
    route = route_ref[...]
    g1, g2 = route[:, 2:3], route[:, 3:4]
    y1 = jnp.concatenate([y_ref[pl.ds(c, tm, stride=2 * rt), :] for c in range(rt)], axis=1)
    y2 = jnp.concatenate([y_ref[pl.ds(rt + c, tm, stride=2 * rt), :] for c in range(rt)], axis=1)
    out_ref[...] = _rmsnorm(h_ref[...] + g1 * y1 + g2 * y2, g_ref[...])


def _combine(y, route, h, g, *, tm):
    t, d = h.shape
    rt = d // V7X_LANES
    return pl.pallas_call(
        _combine_kernel,
        grid=(t // tm,),
        in_specs=[pl.BlockSpec((tm * 2 * rt, V7X_LANES), lambda i: (i, 0)),
                  pl.BlockSpec((tm, V7X_LANES), lambda i: (i, 0)),
                  pl.BlockSpec((tm, d), lambda i: (i, 0)), _const_spec((1, d))],
        out_specs=pl.BlockSpec((tm, d), lambda i: (i, 0)),
        out_shape=jax.ShapeDtypeStruct((t, d), F32),
        compiler_params=_params(("arbitrary",)),
        name="moe_combine_norm",
    )(y, route, h, g)


def _routing_plan(e1, e2, tm):
    flat_e = jnp.stack([e1, e2], axis=1).reshape(-1)
    n_assign = flat_e.shape[0]
    ids = jnp.arange(n_assign, dtype=jnp.int32)
    _, sorted_a = lax.sort((flat_e, ids), num_keys=1, is_stable=True)
    experts = jnp.arange(N_EXPERTS, dtype=jnp.int32)
    counts = jnp.sum((flat_e[:, None] == experts[None, :]).astype(jnp.int32), axis=0)
    seg_end = jnp.cumsum(counts)
    seg_start = seg_end - counts
    first_tile = seg_start // tm
    n_items = jnp.where(counts > 0, (seg_end - 1) // tm - first_tile + 1, 0)
    item_end = jnp.cumsum(n_items)
    item_start = item_end - n_items
    n_work = item_end[-1]
    max_items = n_assign // tm + N_EXPERTS - 1
    w = jnp.minimum(jnp.arange(max_items, dtype=jnp.int32), n_work - 1)
    w_expert = jnp.sum((w[:, None] >= item_end[None, :]).astype(jnp.int32), axis=1)
    pick = lambda v: jnp.sum(jnp.where(w_expert[:, None] == experts[None, :], v[None, :], 0), axis=1)
    w_tile = pick(first_tile) + (w - pick(item_start))
    w_lo = jnp.maximum(pick(seg_start) - w_tile * tm, 0)
    w_hi = jnp.minimum(pick(seg_end) - w_tile * tm, tm)
    prev_tile = jnp.concatenate([jnp.full((1,), -1, jnp.int32), w_tile[:-1]])
    next_tile = jnp.concatenate([w_tile[1:], jnp.full((1,), -1, jnp.int32)])
    is_last_item = jnp.arange(max_items) == n_work - 1
    w_first = (w_tile != prev_tile).astype(jnp.int32)
    w_last = ((w_tile != next_tile) | is_last_item).astype(jnp.int32)
    i32 = lambda v: v.astype(jnp.int32)
    plan = (i32(w_tile), i32(w_expert), w_first, w_last, i32(w_lo), i32(w_hi), i32(n_work).reshape(1))
    return plan, sorted_a


def _rope_tables(length):
    pos = jnp.arange(length, dtype=F32)
    inv = ROPE_THETA ** (-jnp.arange(0, HEAD_DIM, 2, dtype=F32) / HEAD_DIM)
    ang = pos[:, None] * inv[None, :]
    cos, sin = jnp.cos(ang), jnp.sin(ang)
    reps = V7X_LANES // HEAD_DIM
    cos_l = jnp.tile(jnp.concatenate([cos, cos], axis=1), (1, reps))
    sin_l = jnp.tile(jnp.concatenate([-sin, sin], axis=1), (1, reps))
    return cos_l, sin_l


def _expand_kv_cols(w, n_kv):
    lead = w.shape[:-1]
    w = w.reshape(lead + (n_kv, 1, HEAD_DIM))
    w = jnp.broadcast_to(w, lead + (n_kv, GROUP, HEAD_DIM))
    return w.reshape(lead + (n_kv * GROUP * HEAD_DIM,))


def kernel(x, meta_tokens, conv_norm, conv_w_in, conv_w, conv_w_out, ffn_norm, ffn_w_gate, ffn_w_up, ffn_w_down, attn_norm, attn_w_qkv, attn_b_qkv, attn_sinks, attn_w_o, moe_norm, moe_w_router, moe_w_gate, moe_w_up, moe_w_down, final_norm):
    bsz, seq, d = x.shape
    n_meta = meta_tokens.shape[0]
    depth = conv_norm.shape[0] + attn_norm.shape[0]
    assert depth == 2 and conv_norm.shape[0] == 1 and attn_norm.shape[0] == 1
    assert n_meta % V7X_SUBLANES == 0 and n_meta <= BLOCK and n_meta >= CONV_WIDTH - 1
    q_dim = d
    kv_dim = (attn_w_qkv.shape[2] - q_dim) // 2
    n_kv = kv_dim // HEAD_DIM
    assert q_dim == n_kv * GROUP * HEAD_DIM

    tm = 512
    assert seq % tm == 0 and tm % BLOCK == 0
    tiles_per_seq = seq // tm
    t = bsz * seq
    row = lambda a: a.reshape(1, -1).astype(F32)

    h = x.reshape(t, d)
    hm = meta_tokens.astype(F32)

    w_in = conv_w_in[0].astype(BF16)
    w_out = conv_w_out[0].astype(BF16)
    zero_carry = jnp.zeros((V7X_SUBLANES, d), F32)
    hm, vtail_m = _conv_mixer(hm, zero_carry, row(conv_norm[0]), w_in, conv_w[0], w_out,
                              tm=n_meta, tiles_per_seq=1)
    h, _ = _conv_mixer(h, vtail_m, row(conv_norm[0]), w_in, conv_w[0], w_out,
                       tm=tm, tiles_per_seq=tiles_per_seq)
    wg, wu, wd = ffn_w_gate[0].astype(BF16), ffn_w_up[0].astype(BF16), ffn_w_down[0].astype(BF16)
    hm = _ffn(hm, row(ffn_norm[0]), wg, wu, wd, tm=n_meta)
    h = _ffn(h, row(ffn_norm[0]), wg, wu, wd, tm=tm)

    w_qkv, b_qkv = attn_w_qkv[0], attn_b_qkv[0]
    w_exp = jnp.concatenate([w_qkv[:, :q_dim],
                             _expand_kv_cols(w_qkv[:, q_dim:q_dim + kv_dim], n_kv),
                             _expand_kv_cols(w_qkv[:, q_dim + kv_dim:], n_kv)], axis=1).astype(BF16)
    b_exp = jnp.concatenate([b_qkv[:q_dim],
                             _expand_kv_cols(b_qkv[q_dim:q_dim + kv_dim], n_kv),
                             _expand_kv_cols(b_qkv[q_dim + kv_dim:], n_kv)]).reshape(1, -1).astype(F32)
    cos_l, sin_l = _rope_tables(n_meta + seq)
    _, km, vm = _qkv_rope(hm, row(attn_norm[0]), w_exp, b_exp, cos_l[:n_meta], sin_l[:n_meta],
                          tm=n_meta, tiles_per_seq=1)
    q, k, v = _qkv_rope(h, row(attn_norm[0]), w_exp, b_exp, cos_l[n_meta:], sin_l[n_meta:],
                        tm=tm, tiles_per_seq=tiles_per_seq)
    k0 = jnp.zeros((BLOCK, d), BF16).at[BLOCK - n_meta:].set(km)
    v0 = jnp.zeros((BLOCK, d), BF16).at[BLOCK - n_meta:].set(vm)
    h = _attention(attn_sinks[0].astype(F32), q, k, v, k0, v0, h, attn_w_o[0].astype(BF16),
                   tq=tm, tiles_per_seq=tiles_per_seq, first_block_min_key=BLOCK - n_meta)

    w_r = jnp.zeros((d, V7X_LANES), F32).at[:, :N_EXPERTS].set(moe_w_router[0].astype(F32))
    xt, route = _router(h, row(moe_norm[0]), w_r, tm=tm)
    e1 = route[:, 0].astype(jnp.int32)
    e2 = route[:, 1].astype(jnp.int32)
    plan, sorted_a = _routing_plan(e1, e2, tm)
    y = _experts(plan, sorted_a, xt, moe_w_gate[0].astype(BF16), moe_w_up[0].astype(BF16),
                 moe_w_down[0].astype(BF16), tm=tm)
    out = _combine(y, route, h, row(final_norm), tm=tm)
    return out.reshape(bsz, seq, d)
```

```python
import functools

import jax
import jax.numpy as jnp
from jax import lax
from jax.experimental import pallas as pl
from jax.experimental.pallas import tpu as pltpu

CONV_WIDTH = 3
HEAD_DIM = 64
GROUP = 4
WINDOW = 128
BLOCK = 128
ROPE_THETA = 10000.0
N_EXPERTS = 8
RMS_EPS = 1e-5
NEG_BIG = -1e30

V7X_LANES = 128
V7X_SUBLANES = 8
VMEM_LIMIT = 56 * 1024 * 1024

BF16 = jnp.bfloat16
F32 = jnp.float32


def _dot(a, b):
    return jnp.dot(a, b, preferred_element_type=F32)


def _rmsnorm(x, g):
    ms = jnp.mean(x * x, axis=-1, keepdims=True)
    return x * lax.rsqrt(ms + RMS_EPS) * g


def _silu(g):
    return g * (1.0 / (1.0 + jnp.exp(-g)))


def _const_spec(shape):
    nd = len(shape)
    return pl.BlockSpec(shape, lambda *_: (0,) * nd, pipeline_mode=pl.Buffered(1))


def _params(sem):
    return pltpu.CompilerParams(dimension_semantics=sem, vmem_limit_bytes=VMEM_LIMIT)


def _conv_mixer_kernel(h_ref, carry0_ref, g_ref, win_ref, cw_ref, wout_ref, out_ref, vtail_ref,
                       carry_ref, *, tiles_per_seq):
    d = h_ref.shape[1]
    tm = h_ref.shape[0]

    @pl.when(pl.program_id(0) % tiles_per_seq == 0)
    def _():
        carry_ref[...] = carry0_ref[...]

    h = h_ref[...]
    xn = _rmsnorm(h, g_ref[...]).astype(BF16)
    b_gate = _dot(xn, win_ref[:, 0:d])
    c_gate = _dot(xn, win_ref[:, d:2 * d])
    u = _dot(xn, win_ref[:, 2 * d:3 * d])
    v = c_gate * u
    carry = carry_ref[...]
    row = lax.broadcasted_iota(jnp.int32, (V7X_SUBLANES, d), 0)
    r1 = pltpu.roll(v, 1, axis=0)
    r2 = pltpu.roll(v, 2, axis=0)
    head1 = jnp.where(row < 1, pltpu.roll(carry, 1, axis=0), r1[0:V7X_SUBLANES])
    head2 = jnp.where(row < 2, pltpu.roll(carry, 2, axis=0), r2[0:V7X_SUBLANES])
    v1 = jnp.concatenate([head1, r1[V7X_SUBLANES:]], axis=0)
    v2 = jnp.concatenate([head2, r2[V7X_SUBLANES:]], axis=0)
    conv = cw_ref[0:1, :] * v2 + cw_ref[1:2, :] * v1 + cw_ref[2:3, :] * v
    y = (b_gate * conv).astype(BF16)
    out_ref[...] = h + _dot(y, wout_ref[...])
    tail = v[tm - V7X_SUBLANES:tm]
    carry_ref[...] = tail
    vtail_ref[...] = tail


def _conv_mixer(h, carry0, g, w_in, conv_w, w_out, *, tm, tiles_per_seq):
    t, d = h.shape
    nt = t // tm
    return pl.pallas_call(
        functools.partial(_conv_mixer_kernel, tiles_per_seq=tiles_per_seq),
        grid=(nt,),
        in_specs=[
            pl.BlockSpec((tm, d), lambda i: (i, 0)),
            _const_spec((V7X_SUBLANES, d)),
            _const_spec((1, d)),
            _const_spec((d, 3 * d)),
            _const_spec((CONV_WIDTH, d)),
            _const_spec((d, d)),
        ],
        out_specs=[
            pl.BlockSpec((tm, d), lambda i: (i, 0)),
            pl.BlockSpec((V7X_SUBLANES, d), lambda i: (i, 0)),
        ],
        out_shape=[
            jax.ShapeDtypeStruct((t, d), F32),
            jax.ShapeDtypeStruct((nt * V7X_SUBLANES, d), F32),
        ],
        scratch_shapes=[pltpu.VMEM((V7X_SUBLANES, d), F32)],
        compiler_params=_params(("arbitrary",)),
        name="conv_mixer",
    )(h, carry0, g, w_in, conv_w, w_out)


def _ff_chunks(total, chunk):
    out, s = [], 0
    while s < total:
        out.append((s, min(chunk, total - s)))
        s += chunk
    return out


def _swiglu_partial(x, wg_ref, wu_ref, wd_ref, chunk, after_chunk=None):
    acc = None
    pieces = _ff_chunks(wg_ref.shape[1], chunk)
    for i, (s, n) in enumerate(pieces):
        g = _dot(x, wg_ref[:, s:s + n])
        u = _dot(x, wu_ref[:, s:s + n])
        a = (_silu(g) * u).astype(BF16)
        part = _dot(a, wd_ref[s:s + n, :])
        acc = part if acc is None else acc + part
        if after_chunk is not None:
            after_chunk(i, len(pieces))
    return acc


def _ffn_kernel(h_ref, g_ref, wg_ref, wu_ref, wd_ref, out_ref, *, chunk):
    h = h_ref[...]
    xn = _rmsnorm(h, g_ref[...]).astype(BF16)
    out_ref[...] = h + _swiglu_partial(xn, wg_ref, wu_ref, wd_ref, chunk)


def _ffn(h, g, wg, wu, wd, *, tm, chunk=512):
    t, d = h.shape
    f = wg.shape[1]
    return pl.pallas_call(
        functools.partial(_ffn_kernel, chunk=chunk),
        grid=(t // tm,),
        in_specs=[
            pl.BlockSpec((tm, d), lambda i: (i, 0)),
            _const_spec((1, d)),
            _const_spec((d, f)),
            _const_spec((d, f)),
            _const_spec((f, d)),
        ],
        out_specs=pl.BlockSpec((tm, d), lambda i: (i, 0)),
        out_shape=jax.ShapeDtypeStruct((t, d), F32),
        compiler_params=_params(("arbitrary",)),
        name="dense_ffn",
    )(h, g, wg, wu, wd)


def _rope(x, cos, sin_signed, first_half):
    outs = []
    for j in range(x.shape[1] // V7X_LANES):
        xj = x[:, j * V7X_LANES:(j + 1) * V7X_LANES]
        partner = jnp.where(first_half, pltpu.roll(xj, V7X_LANES - HEAD_DIM // 2, axis=1),
                            pltpu.roll(xj, HEAD_DIM // 2, axis=1))
        outs.append(xj * cos + partner * sin_signed)
    return jnp.concatenate(outs, axis=1)


def _repeat_heads(x, lane):
    half = V7X_LANES // 2
    assert HEAD_DIM == half and GROUP * HEAD_DIM == 2 * V7X_LANES
    outs = []
    for j in range(x.shape[1] // V7X_LANES):
        xj = x[:, j * V7X_LANES:(j + 1) * V7X_LANES]
        swapped = pltpu.roll(xj, half, axis=1)
        even = jnp.where(lane < half, xj, swapped)
        odd = jnp.where(lane < half, swapped, xj)
        outs += [even, even, odd, odd]
    return jnp.concatenate(outs, axis=1)


def _project_qkv(h, g_ref, w_ref, b_ref, cos, sin_signed, d):
    xn = _rmsnorm(h, g_ref[...]).astype(BF16)
    lane = lax.broadcasted_iota(jnp.int32, cos.shape, 1)
    first_half = (lane % HEAD_DIM) < (HEAD_DIM // 2)
    kvw = (w_ref.shape[1] - d) // 2
    q = _dot(xn, w_ref[:, 0:d]) + b_ref[:, 0:d]
    k = _dot(xn, w_ref[:, d:d + kvw]) + b_ref[:, d:d + kvw]
    v = _dot(xn, w_ref[:, d + kvw:d + 2 * kvw]) + b_ref[:, d + kvw:d + 2 * kvw]
    q = (_rope(q, cos, sin_signed, first_half) * (HEAD_DIM ** -0.5)).astype(BF16)
    k = _repeat_heads(_rope(k, cos, sin_signed, first_half), lane).astype(BF16)
    v = _repeat_heads(v, lane).astype(BF16)
    return q, k, v


def _kv_kernel(h_ref, g_ref, w_ref, b_ref, cos_ref, sin_ref, k_ref, v_ref):
    _, k, v = _project_qkv(h_ref[...], g_ref, w_ref, b_ref, cos_ref[...], sin_ref[...], h_ref.shape[1])
    k_ref[...] = k
    v_ref[...] = v


def _kv_rope(h, g, w_qkv, b_qkv, cos, sin_signed):
    t, d = h.shape
    n = w_qkv.shape[1]
    return pl.pallas_call(
        _kv_kernel,
        grid=(1,),
        in_specs=[_const_spec((t, d)), _const_spec((1, d)), _const_spec((d, n)), _const_spec((1, n)),
                  _const_spec((t, V7X_LANES)), _const_spec((t, V7X_LANES))],
        out_specs=[pl.BlockSpec((t, d), lambda i: (0, 0))] * 2,
        out_shape=[jax.ShapeDtypeStruct((t, d), BF16)] * 2,
        compiler_params=_params(("arbitrary",)),
        name="meta_kv_rope",
    )(h, g, w_qkv, b_qkv, cos, sin_signed)


def _attn_kernel(sinks_ref, h_ref, g_ref, wqkv_ref, bqkv_ref, cos_ref, sin_ref, k0_ref, v0_ref, wo_ref,
                 g_moe_ref, wr_ref, out_ref, xt_ref, route_ref,
                 q_ref, k_ref, v_ref, kprev_ref, vprev_ref, o_ref, *, tiles_per_seq, first_block_min_key):
    tq, d = h_ref.shape
    nblk = tq // BLOCK
    n_kv = d // (GROUP * HEAD_DIM)
    hw = GROUP * HEAD_DIM
    first = (pl.program_id(0) % tiles_per_seq) == 0

    @pl.when(first)
    def _():
        kprev_ref[...] = k0_ref[...]
        vprev_ref[...] = v0_ref[...]

    q_ref[...], k_ref[...], v_ref[...] = _project_qkv(h_ref[...], g_ref, wqkv_ref, bqkv_ref, cos_ref[...],
                                                      sin_ref[...], d)

    jmin = jnp.where(first, first_block_min_key, 0)
    rows = GROUP * BLOCK
    r_idx = lax.broadcasted_iota(jnp.int32, (rows, 2 * BLOCK), 0) % BLOCK
    j_idx = lax.broadcasted_iota(jnp.int32, (rows, 2 * BLOCK), 1)
    band = (j_idx > r_idx) & (j_idx <= r_idx + WINDOW)
    band_first = band & (j_idx >= jmin)
    row_grp = lax.broadcasted_iota(jnp.int32, (rows, 1), 0) // BLOCK
    lane_grp = lax.broadcasted_iota(jnp.int32, (BLOCK, hw), 1) // HEAD_DIM
    grp_mask = [lane_grp == g for g in range(GROUP)]
    grp_mask_bf = [m.astype(BF16) for m in grp_mask]

    for b in range(nblk):
        cur = slice(b * BLOCK, (b + 1) * BLOCK)
        if b == 0:
            kp, vp, valid = kprev_ref[...], vprev_ref[...], band_first
        else:
            prev = slice((b - 1) * BLOCK, b * BLOCK)
            kp, vp, valid = k_ref[prev, :], v_ref[prev, :], band
        kw = jnp.concatenate([kp, k_ref[cur, :]], axis=0)
        vw = jnp.concatenate([vp, v_ref[cur, :]], axis=0)
        qb = q_ref[cur, :]
        outs = []
        for hh in range(n_kv):
            ls = slice(hh * hw, (hh + 1) * hw)
            q_h, kw_h, vw_h = qb[:, ls], kw[:, ls], vw[:, ls]
            qs = jnp.concatenate([q_h * grp_mask_bf[g] for g in range(GROUP)], axis=0)
            s = lax.dot_general(qs, kw_h, (((1,), (1,)), ((), ())), preferred_element_type=F32)
            s = jnp.where(valid, s, NEG_BIG)
            sink = jnp.zeros((rows, 1), F32)
            for g in range(GROUP):
                sink = jnp.where(row_grp == g, sinks_ref[hh * GROUP + g], sink)
            m = jnp.maximum(jnp.max(s, axis=-1, keepdims=True), sink)
            p = jnp.exp(s - m)
            denom = jnp.sum(p, axis=-1, keepdims=True) + jnp.exp(sink - m)
            inv = 1.0 / denom
            pb = p.astype(BF16)
            o_h = jnp.zeros((BLOCK, hw), F32)
            for g in range(GROUP):
                rs = slice(g * BLOCK, (g + 1) * BLOCK)
                o_h = jnp.where(grp_mask[g], _dot(pb[rs], vw_h) * inv[rs], o_h)
            outs.append(o_h)
        o_ref[cur, :] = jnp.concatenate(outs, axis=1).astype(BF16)

    last = slice((nblk - 1) * BLOCK, nblk * BLOCK)
    kprev_ref[...] = k_ref[last, :]
    vprev_ref[...] = v_ref[last, :]
    h_out = h_ref[...] + _dot(o_ref[...], wo_ref[...])
    out_ref[...] = h_out
    xn = _rmsnorm(h_out, g_moe_ref[...])
    route_ref[...] = _route(xn, wr_ref[...])
    _to_row_tiles(xt_ref, xn)


def _to_row_tiles(dst_ref, x):
    tm = x.shape[0]
    for c in range(x.shape[1] // V7X_LANES):
        dst_ref[pl.ds(c, tm, stride=V7X_SUBLANES), :] = x[:, c * V7X_LANES:(c + 1) * V7X_LANES]


def _from_row_tiles(src_ref, tm):
    n = src_ref.shape[0] // tm
    return jnp.concatenate([src_ref[pl.ds(c, tm, stride=n), :] for c in range(n)], axis=1)


def _route(xn, w):
    x_hi = xn.astype(BF16)
    x_lo = (xn - x_hi.astype(F32)).astype(BF16)
    w_hi = w.astype(BF16)
    w_lo = (w - w_hi.astype(F32)).astype(BF16)
    logits = _dot(x_hi, w_hi) + (_dot(x_hi, w_lo) + _dot(x_lo, w_hi))
    lane = lax.broadcasted_iota(jnp.int32, logits.shape, 1)
    neg_inf = jnp.float32(-jnp.inf)
    lg = jnp.where(lane < N_EXPERTS, logits, neg_inf)
    m1 = jnp.max(lg, axis=-1, keepdims=True)
    i1 = jnp.min(jnp.where(lg == m1, lane, V7X_LANES), axis=-1, keepdims=True)
    lg2 = jnp.where(lane == i1, neg_inf, lg)
    m2 = jnp.max(lg2, axis=-1, keepdims=True)
    i2 = jnp.min(jnp.where(lg2 == m2, lane, V7X_LANES), axis=-1, keepdims=True)
    e = jnp.exp(m2 - m1)
    g1 = 1.0 / (1.0 + e)
    g2 = e / (1.0 + e)
    return jnp.where(lane == 0, i1.astype(F32),
                     jnp.where(lane == 1, i2.astype(F32),
                               jnp.where(lane == 2, g1, jnp.where(lane == 3, g2, 0.0))))


def _attention_router(sinks, h, g, w_qkv, b_qkv, cos, sin_signed, k0, v0, w_o, g_moe, w_router_padded, *,
                      tq, tiles_per_seq, first_block_min_key):
    t, d = h.shape
    n = w_qkv.shape[1]
    rt = d // V7X_LANES
    tok = pl.BlockSpec((tq, d), lambda i: (i, 0))
    tab = pl.BlockSpec((tq, V7X_LANES), lambda i: (i % tiles_per_seq, 0))
    return pl.pallas_call(
        functools.partial(_attn_kernel, tiles_per_seq=tiles_per_seq,
                          first_block_min_key=first_block_min_key),
        grid=(t // tq,),
        in_specs=[
            pl.BlockSpec(memory_space=pltpu.SMEM),
            tok, _const_spec((1, d)), _const_spec((d, n)), _const_spec((1, n)), tab, tab,
            _const_spec((BLOCK, d)), _const_spec((BLOCK, d)),
            _const_spec((d, d)), _const_spec((1, d)), _const_spec((d, V7X_LANES)),
        ],
        out_specs=[tok, pl.BlockSpec((tq * rt, V7X_LANES), lambda i: (i, 0)),
                   pl.BlockSpec((tq, V7X_LANES), lambda i: (i, 0))],
        out_shape=[jax.ShapeDtypeStruct((t, d), F32),
                   jax.ShapeDtypeStruct((t * rt, V7X_LANES), F32),
                   jax.ShapeDtypeStruct((t, V7X_LANES), F32)],
        scratch_shapes=[pltpu.VMEM((tq, d), BF16), pltpu.VMEM((tq, d), BF16), pltpu.VMEM((tq, d), BF16),
                        pltpu.VMEM((BLOCK, d), BF16), pltpu.VMEM((BLOCK, d), BF16),
                        pltpu.VMEM((tq, d), BF16)],
        compiler_params=_params(("arbitrary",)),
        name="swa_attention_router",
    )(sinks, h, g, w_qkv, b_qkv, cos, sin_signed, k0, v0, w_o, g_moe, w_router_padded)


def _row_copy(src_ref, src_row, dst_ref, dst_row, sem, rt):
    return pltpu.make_async_copy(src_ref.at[pl.ds(pl.multiple_of(src_row * rt, rt), rt), :],
                                 dst_ref.at[pl.ds(pl.multiple_of(dst_row * rt, rt), rt), :], sem)


def _wait_rows(hbm_ref, vmem_ref, sem):
    pltpu.make_async_copy(hbm_ref.at[pl.ds(0, vmem_ref.shape[0]), :], vmem_ref, sem).wait()


def _expert_kernel(w_tile_ref, w_expert_ref, w_first_ref, w_last_ref, w_lo_ref, w_hi_ref, n_work_ref,
                   src0_ref, src_next_ref, dst_prev_ref, dst_cur_ref, xt_hbm, wg_ref, wu_ref, wd_ref, y_hbm,
                   gbuf_ref, xb_ref, acc_ref, stage_ref, gsem, ssem, *, chunk):
    del w_tile_ref, w_expert_ref
    tm, d = xb_ref.shape
    rt = d // V7X_LANES
    w = pl.program_id(0)
    n_work = n_work_ref[0]
    active = w < n_work
    first = active & (w_first_ref[w] == 1)
    last = active & (w_last_ref[w] == 1)
    row_id = lax.broadcasted_iota(jnp.int32, (tm, 1), 0)

    def masked(part):
        return jnp.where((row_id >= w_lo_ref[w]) & (row_id < w_hi_ref[w]), part, 0.0)

    @pl.when(w == 0)
    def _():
        stage_ref[...] = jnp.zeros_like(stage_ref)

        def body(r, c):
            _row_copy(xt_hbm, src0_ref[r], gbuf_ref, r, gsem, rt).start()
            return c
        lax.fori_loop(0, tm, body, 0)

    @pl.when(first)
    def _():
        _wait_rows(xt_hbm, gbuf_ref, gsem)
        xb_ref[...] = _from_row_tiles(gbuf_ref, tm).astype(BF16)

        def side_dmas(i, n):
            half = (n - 1) // 2
            if i < half:
                for r in range(i * tm // half, (i + 1) * tm // half):
                    _row_copy(xt_hbm, src_next_ref[r], gbuf_ref, r, gsem, rt).start()
            elif i < 2 * half:
                j = i - half
                for r in range(j * tm // half, (j + 1) * tm // half):
                    _row_copy(stage_ref, r, y_hbm, dst_prev_ref[r], ssem, rt).start()

        acc_ref[...] = masked(_swiglu_partial(xb_ref[...], wg_ref, wu_ref, wd_ref, chunk, side_dmas))

    @pl.when(active & jnp.logical_not(first))
    def _():
        acc_ref[...] += masked(_swiglu_partial(xb_ref[...], wg_ref, wu_ref, wd_ref, chunk))

    @pl.when(last)
    def _():
        _wait_rows(y_hbm, stage_ref, ssem)
        _to_row_tiles(stage_ref, acc_ref[...])

    @pl.when(w == n_work - 1)
    def _():
        def body(r, c):
            _row_copy(stage_ref, r, y_hbm, dst_cur_ref[r], ssem, rt).start()
            return c
        lax.fori_loop(0, tm, body, 0)
        _wait_rows(y_hbm, stage_ref, ssem)
        _wait_rows(xt_hbm, gbuf_ref, gsem)


def _experts(plan, sorted_a, xt, wg, wu, wd, *, tm, chunk=512):
    n_e, d, f = wg.shape
    rt = d // V7X_LANES
    n_assign = sorted_a.shape[0]
    n_tiles = n_assign // tm
    n_items = n_tiles + n_e - 1
    dst_rows = jnp.concatenate([sorted_a, n_assign + jnp.arange(tm, dtype=jnp.int32)])
    src_rows = jnp.concatenate([sorted_a // 2, jnp.zeros((tm,), jnp.int32)])
    smem_tile = lambda fn: pl.BlockSpec((tm,), fn, memory_space=pltpu.SMEM)
    grid_spec = pltpu.PrefetchScalarGridSpec(
        num_scalar_prefetch=7,
        grid=(n_items,),
        in_specs=[
            smem_tile(lambda w, wt, *_: (0,)),
            smem_tile(lambda w, wt, *_: (wt[w] + 1,)),
            smem_tile(lambda w, wt, *_: (jnp.where(wt[w] == 0, n_tiles, wt[w] - 1),)),
            smem_tile(lambda w, wt, *_: (wt[w],)),
            pl.BlockSpec(memory_space=pl.ANY),
            pl.BlockSpec((None, d, f), lambda w, wt, we, *_: (we[w], 0, 0)),
            pl.BlockSpec((None, d, f), lambda w, wt, we, *_: (we[w], 0, 0)),
            pl.BlockSpec((None, f, d), lambda w, wt, we, *_: (we[w], 0, 0)),
        ],
        out_specs=pl.BlockSpec(memory_space=pl.ANY),
        scratch_shapes=[pltpu.VMEM((tm * rt, V7X_LANES), F32), pltpu.VMEM((tm, d), BF16),
                        pltpu.VMEM((tm, d), F32), pltpu.VMEM((tm * rt, V7X_LANES), F32),
                        pltpu.SemaphoreType.DMA, pltpu.SemaphoreType.DMA],
    )
    return pl.pallas_call(
        functools.partial(_expert_kernel, chunk=chunk),
        grid_spec=grid_spec,
        out_shape=jax.ShapeDtypeStruct(((n_assign + tm) * rt, V7X_LANES), F32),
        compiler_params=_params(("arbitrary",)),
        name="moe_experts",
    )(*plan, src_rows, src_rows, dst_rows, dst_rows, xt, wg, wu, wd)


def _combine_kernel(y_ref, route_ref, h_ref, g_ref, out_ref):
    tm, d = h_ref.shape
    rt = d // V7X_LANES
    route = route_ref[...]
    g1, g2 = route[:, 2:3], route[:, 3:4]
    y1 = jnp.concatenate([y_ref[pl.ds(c, tm, stride=2 * rt), :] for c in range(rt)], axis=1)
    y2 = jnp.concatenate([y_ref[pl.ds(rt + c, tm, stride=2 * rt), :] for c in range(rt)], axis=1)
    out_ref[...] = _rmsnorm(h_ref[...] + g1 * y1 + g2 * y2, g_ref[...])


def _combine(y, route, h, g, *, tm):
    t, d = h.shape
    rt = d // V7X_LANES
    return pl.pallas_call(
        _combine_kernel,
        grid=(t // tm,),
        in_specs=[pl.BlockSpec((tm * 2 * rt, V7X_LANES), lambda i: (i, 0)),
                  pl.BlockSpec((tm, V7X_LANES), lambda i: (i, 0)),
                  pl.BlockSpec((tm, d), lambda i: (i, 0)), _const_spec((1, d))],
        out_specs=pl.BlockSpec((tm, d), lambda i: (i, 0)),
        out_shape=jax.ShapeDtypeStruct((t, d), F32),
        compiler_params=_params(("arbitrary",)),
        name="moe_combine_norm",
    )(y, route, h, g)


def _routing_plan(e1, e2, tm):
    flat_e = jnp.stack([e1, e2], axis=1).reshape(-1)
    n_assign = flat_e.shape[0]
    ids = jnp.arange(n_assign, dtype=jnp.int32)
    _, sorted_a = lax.sort((flat_e, ids), num_keys=1, is_stable=True)
    experts = jnp.arange(N_EXPERTS, dtype=jnp.int32)
    counts = jnp.sum((flat_e[:, None] == experts[None, :]).astype(jnp.int32), axis=0)
    seg_end = jnp.cumsum(counts)
    seg_start = seg_end - counts
    first_tile = seg_start // tm
    n_items = jnp.where(counts > 0, (seg_end - 1) // tm - first_tile + 1, 0)
    item_end = jnp.cumsum(n_items)
    item_start = item_end - n_items
    n_work = item_end[-1]
    max_items = n_assign // tm + N_EXPERTS - 1
    w = jnp.minimum(jnp.arange(max_items, dtype=jnp.int32), n_work - 1)
    w_expert = jnp.sum((w[:, None] >= item_end[None, :]).astype(jnp.int32), axis=1)
    pick = lambda v: jnp.sum(jnp.where(w_expert[:, None] == experts[None, :], v[None, :], 0), axis=1)
    w_tile = pick(first_tile) + (w - pick(item_start))
    w_lo = jnp.maximum(pick(seg_start) - w_tile * tm, 0)
    w_hi = jnp.minimum(pick(seg_end) - w_tile * tm, tm)
    prev_tile = jnp.concatenate([jnp.full((1,), -1, jnp.int32), w_tile[:-1]])
    next_tile = jnp.concatenate([w_tile[1:], jnp.full((1,), -1, jnp.int32)])
    is_last_item = jnp.arange(max_items) == n_work - 1
    w_first = (w_tile != prev_tile).astype(jnp.int32)
    w_last = ((w_tile != next_tile) | is_last_item).astype(jnp.int32)
    i32 = lambda v: v.astype(jnp.int32)
    plan = (i32(w_tile), i32(w_expert), w_first, w_last, i32(w_lo), i32(w_hi), i32(n_work).reshape(1))
    return plan, sorted_a


def _rope_tables(length):
    pos = jnp.arange(length, dtype=F32)
    inv = ROPE_THETA ** (-jnp.arange(0, HEAD_DIM, 2, dtype=F32) / HEAD_DIM)
    ang = pos[:, None] * inv[None, :]
    cos, sin = jnp.cos(ang), jnp.sin(ang)
    reps = V7X_LANES // HEAD_DIM
    cos_l = jnp.tile(jnp.concatenate([cos, cos], axis=1), (1, reps))
    sin_l = jnp.tile(jnp.concatenate([-sin, sin], axis=1), (1, reps))
    return cos_l, sin_l


def kernel(x, meta_tokens, conv_norm, conv_w_in, conv_w, conv_w_out, ffn_norm, ffn_w_gate, ffn_w_up, ffn_w_down, attn_norm, attn_w_qkv, attn_b_qkv, attn_sinks, attn_w_o, moe_norm, moe_w_router, moe_w_gate, moe_w_up, moe_w_down, final_norm):
    bsz, seq, d = x.shape
    n_meta = meta_tokens.shape[0]
    depth = conv_norm.shape[0] + attn_norm.shape[0]
    assert depth == 2 and conv_norm.shape[0] == 1 and attn_norm.shape[0] == 1
    assert n_meta % V7X_SUBLANES == 0 and n_meta <= BLOCK and n_meta >= CONV_WIDTH - 1
    q_dim = d
    kv_dim = (attn_w_qkv.shape[2] - q_dim) // 2
    n_kv = kv_dim // HEAD_DIM
    assert q_dim == n_kv * GROUP * HEAD_DIM

    tm = 512
    assert seq % tm == 0 and tm % BLOCK == 0
    tiles_per_seq = seq // tm
    t = bsz * seq
    row = lambda a: a.reshape(1, -1).astype(F32)

    h = x.reshape(t, d)
    hm = meta_tokens.astype(F32)

    w_in = conv_w_in[0].astype(BF16)
    w_out = conv_w_out[0].astype(BF16)
    zero_carry = jnp.zeros((V7X_SUBLANES, d), F32)
    hm, vtail_m = _conv_mixer(hm, zero_carry, row(conv_norm[0]), w_in, conv_w[0], w_out,
                              tm=n_meta, tiles_per_seq=1)
    h, _ = _conv_mixer(h, vtail_m, row(conv_norm[0]), w_in, conv_w[0], w_out,
                       tm=tm, tiles_per_seq=tiles_per_seq)
    wg, wu, wd = ffn_w_gate[0].astype(BF16), ffn_w_up[0].astype(BF16), ffn_w_down[0].astype(BF16)
    hm = _ffn(hm, row(ffn_norm[0]), wg, wu, wd, tm=n_meta)
    h = _ffn(h, row(ffn_norm[0]), wg, wu, wd, tm=tm)

    w_qkv, b_qkv = attn_w_qkv[0].astype(BF16), row(attn_b_qkv[0])
    cos_l, sin_l = _rope_tables(n_meta + seq)
    km, vm = _kv_rope(hm, row(attn_norm[0]), w_qkv, b_qkv, cos_l[:n_meta], sin_l[:n_meta])
    k0 = jnp.zeros((BLOCK, d), BF16).at[BLOCK - n_meta:].set(km)
    v0 = jnp.zeros((BLOCK, d), BF16).at[BLOCK - n_meta:].set(vm)
    w_r = jnp.zeros((d, V7X_LANES), F32).at[:, :N_EXPERTS].set(moe_w_router[0].astype(F32))
    h, xt, route = _attention_router(
        attn_sinks[0].astype(F32), h, row(attn_norm[0]), w_qkv, b_qkv, cos_l[n_meta:], sin_l[n_meta:],
        k0, v0, attn_w_o[0].astype(BF16), row(moe_norm[0]), w_r,
        tq=tm, tiles_per_seq=tiles_per_seq, first_block_min_key=BLOCK - n_meta)

    e1 = route[:, 0].astype(jnp.int32)
    e2 = route[:, 1].astype(jnp.int32)
    plan, sorted_a = _routing_plan(e1, e2, tm)
    y = _experts(plan, sorted_a, xt, moe_w_gate[0].astype(BF16), moe_w_up[0].astype(BF16),
                 moe_w_down[0].astype(BF16), tm=tm)
    out = _combine(y, route, h, row(final_norm), tm=tm)
    return out.reshape(bsz, seq, d)
```

```python
import functools

import jax
import jax.numpy as jnp
from jax import lax
from jax.experimental import pallas as pl
from jax.experimental.pallas import tpu as pltpu

CONV_WIDTH = 3
HEAD_DIM = 64
GROUP = 4
WINDOW = 128
BLOCK = 128
ROPE_THETA = 10000.0
N_EXPERTS = 8
RMS_EPS = 1e-5
NEG_BIG = -1e30

V7X_LANES = 128
V7X_SUBLANES = 8
VMEM_LIMIT = 56 * 1024 * 1024

BF16 = jnp.bfloat16
F32 = jnp.float32


def _dot(a, b):
    return jnp.dot(a, b, preferred_element_type=F32)


def _rmsnorm(x, g):
    ms = jnp.mean(x * x, axis=-1, keepdims=True)
    return x * lax.rsqrt(ms + RMS_EPS) * g


def _silu(g):
    return g * (1.0 / (1.0 + jnp.exp(-g)))


def _const_spec(shape):
    nd = len(shape)
    return pl.BlockSpec(shape, lambda *_: (0,) * nd, pipeline_mode=pl.Buffered(1))


def _params(sem):
    return pltpu.CompilerParams(dimension_semantics=sem, vmem_limit_bytes=VMEM_LIMIT)


def _conv_mixer_kernel(h_ref, carry0_ref, g_ref, win_ref, cw_ref, wout_ref, out_ref, vtail_ref,
                       carry_ref, *, tiles_per_seq):
    d = h_ref.shape[1]
    tm = h_ref.shape[0]

    @pl.when(pl.program_id(0) % tiles_per_seq == 0)
    def _():
        carry_ref[...] = carry0_ref[...]

    h = h_ref[...]
    xn = _rmsnorm(h, g_ref[...]).astype(BF16)
    b_gate = _dot(xn, win_ref[:, 0:d])
    c_gate = _dot(xn, win_ref[:, d:2 * d])
    u = _dot(xn, win_ref[:, 2 * d:3 * d])
    v = c_gate * u
    carry = carry_ref[...]
    row = lax.broadcasted_iota(jnp.int32, (V7X_SUBLANES, d), 0)
    r1 = pltpu.roll(v, 1, axis=0)
    r2 = pltpu.roll(v, 2, axis=0)
    head1 = jnp.where(row < 1, pltpu.roll(carry, 1, axis=0), r1[0:V7X_SUBLANES])
    head2 = jnp.where(row < 2, pltpu.roll(carry, 2, axis=0), r2[0:V7X_SUBLANES])
    v1 = jnp.concatenate([head1, r1[V7X_SUBLANES:]], axis=0)
    v2 = jnp.concatenate([head2, r2[V7X_SUBLANES:]], axis=0)
    conv = cw_ref[0:1, :] * v2 + cw_ref[1:2, :] * v1 + cw_ref[2:3, :] * v
    y = (b_gate * conv).astype(BF16)
    out_ref[...] = h + _dot(y, wout_ref[...])
    tail = v[tm - V7X_SUBLANES:tm]
    carry_ref[...] = tail
    vtail_ref[...] = tail


def _conv_mixer(h, carry0, g, w_in, conv_w, w_out, *, tm, tiles_per_seq):
    t, d = h.shape
    nt = t // tm
    return pl.pallas_call(
        functools.partial(_conv_mixer_kernel, tiles_per_seq=tiles_per_seq),
        grid=(nt,),
        in_specs=[
            pl.BlockSpec((tm, d), lambda i: (i, 0)),
            _const_spec((V7X_SUBLANES, d)),
            _const_spec((1, d)),
            _const_spec((d, 3 * d)),
            _const_spec((CONV_WIDTH, d)),
            _const_spec((d, d)),
        ],
        out_specs=[
            pl.BlockSpec((tm, d), lambda i: (i, 0)),
            pl.BlockSpec((V7X_SUBLANES, d), lambda i: (i, 0)),
        ],
        out_shape=[
            jax.ShapeDtypeStruct((t, d), F32),
            jax.ShapeDtypeStruct((nt * V7X_SUBLANES, d), F32),
        ],
        scratch_shapes=[pltpu.VMEM((V7X_SUBLANES, d), F32)],
        compiler_params=_params(("arbitrary",)),
        name="conv_mixer",
    )(h, carry0, g, w_in, conv_w, w_out)


def _ff_chunks(total, chunk):
    out, s = [], 0
    while s < total:
        out.append((s, min(chunk, total - s)))
        s += chunk
    return out


def _swiglu_partial(x, wg_ref, wu_ref, wd_ref, chunk, after_chunk=None):
    acc = None
    pieces = _ff_chunks(wg_ref.shape[1], chunk)
    for i, (s, n) in enumerate(pieces):
        g = _dot(x, wg_ref[:, s:s + n])
        u = _dot(x, wu_ref[:, s:s + n])
        a = (_silu(g) * u).astype(BF16)
        part = _dot(a, wd_ref[s:s + n, :])
        acc = part if acc is None else acc + part
        if after_chunk is not None:
            after_chunk(i, len(pieces))
    return acc


def _ffn_kernel(h_ref, g_ref, wg_ref, wu_ref, wd_ref, *rest, chunk, n_cast):
    cast_in, out_ref, cast_out = rest[:n_cast], rest[n_cast], rest[n_cast + 1:]
    h = h_ref[...]
    xn = _rmsnorm(h, g_ref[...]).astype(BF16)
    out_ref[...] = h + _swiglu_partial(xn, wg_ref, wu_ref, wd_ref, chunk)
    for src, dst in zip(cast_in, cast_out):
        dst[...] = src[...].astype(BF16)


def _ffn(h, g, wg, wu, wd, *, tm, chunk=512, cast=()):
    t, d = h.shape
    f = wg.shape[1]
    nt = t // tm
    slab = lambda a: pl.BlockSpec((a.shape[0] // nt, a.shape[1]), lambda i: (i, 0))
    for a in cast:
        assert a.shape[0] % (nt * 2 * V7X_SUBLANES) == 0
    outs = pl.pallas_call(
        functools.partial(_ffn_kernel, chunk=chunk, n_cast=len(cast)),
        grid=(nt,),
        in_specs=[
            pl.BlockSpec((tm, d), lambda i: (i, 0)),
            _const_spec((1, d)),
            _const_spec((d, f)),
            _const_spec((d, f)),
            _const_spec((f, d)),
        ] + [slab(a) for a in cast],
        out_specs=[pl.BlockSpec((tm, d), lambda i: (i, 0))] + [slab(a) for a in cast],
        out_shape=[jax.ShapeDtypeStruct((t, d), F32)] + [jax.ShapeDtypeStruct(a.shape, BF16) for a in cast],
        compiler_params=_params(("arbitrary",)),
        name="dense_ffn",
    )(h, g, wg, wu, wd, *cast)
    return outs[0], outs[1:]


def _rope(x, cos, sin_signed, first_half):
    outs = []
    for j in range(x.shape[1] // V7X_LANES):
        xj = x[:, j * V7X_LANES:(j + 1) * V7X_LANES]
        partner = jnp.where(first_half, pltpu.roll(xj, V7X_LANES - HEAD_DIM // 2, axis=1),
                            pltpu.roll(xj, HEAD_DIM // 2, axis=1))
        outs.append(xj * cos + partner * sin_signed)
    return jnp.concatenate(outs, axis=1)


def _repeat_heads(x, lane):
    half = V7X_LANES // 2
    assert HEAD_DIM == half and GROUP * HEAD_DIM == 2 * V7X_LANES
    outs = []
    for j in range(x.shape[1] // V7X_LANES):
        xj = x[:, j * V7X_LANES:(j + 1) * V7X_LANES]
        swapped = pltpu.roll(xj, half, axis=1)
        even = jnp.where(lane < half, xj, swapped)
        odd = jnp.where(lane < half, swapped, xj)
        outs += [even, even, odd, odd]
    return jnp.concatenate(outs, axis=1)


def _project_qkv(h, g_ref, w_ref, b_ref, cos, sin_signed, d):
    xn = _rmsnorm(h, g_ref[...]).astype(BF16)
    lane = lax.broadcasted_iota(jnp.int32, cos.shape, 1)
    first_half = (lane % HEAD_DIM) < (HEAD_DIM // 2)
    kvw = (w_ref.shape[1] - d) // 2
    q = _dot(xn, w_ref[:, 0:d]) + b_ref[:, 0:d]
    k = _dot(xn, w_ref[:, d:d + kvw]) + b_ref[:, d:d + kvw]
    v = _dot(xn, w_ref[:, d + kvw:d + 2 * kvw]) + b_ref[:, d + kvw:d + 2 * kvw]
    q = (_rope(q, cos, sin_signed, first_half) * (HEAD_DIM ** -0.5)).astype(BF16)
    k = _repeat_heads(_rope(k, cos, sin_signed, first_half), lane).astype(BF16)
    v = _repeat_heads(v, lane).astype(BF16)
    return q, k, v


def _qkv_kernel(h_ref, g_ref, w_ref, b_ref, cos_ref, sin_ref, q_ref, k_ref, v_ref):
    q_ref[...], k_ref[...], v_ref[...] = _project_qkv(h_ref[...], g_ref, w_ref, b_ref, cos_ref[...],
                                                      sin_ref[...], h_ref.shape[1])


def _qkv_rope(h, g, w_qkv, b_qkv, cos, sin_signed, *, tm, tiles_per_seq):
    t, d = h.shape
    n = w_qkv.shape[1]
    tok = pl.BlockSpec((tm, d), lambda i: (i, 0))
    tab = pl.BlockSpec((tm, V7X_LANES), lambda i: (i % tiles_per_seq, 0))
    return pl.pallas_call(
        _qkv_kernel,
        grid=(t // tm,),
        in_specs=[tok, _const_spec((1, d)), _const_spec((d, n)), _const_spec((1, n)), tab, tab],
        out_specs=[tok, tok, tok],
        out_shape=[jax.ShapeDtypeStruct((t, d), BF16)] * 3,
        compiler_params=_params(("arbitrary",)),
        name="qkv_rope",
    )(h, g, w_qkv, b_qkv, cos, sin_signed)


def _attn_kernel(sinks_ref, q_ref, k_ref, v_ref, k0_ref, v0_ref, h_ref, wo_ref, out_ref,
                 kprev_ref, vprev_ref, o_ref, *, tiles_per_seq, first_block_min_key):
    tq, d = q_ref.shape
    nblk = tq // BLOCK
    n_kv = d // (GROUP * HEAD_DIM)
    hw = GROUP * HEAD_DIM
    first = (pl.program_id(0) % tiles_per_seq) == 0

    @pl.when(first)
    def _():
        kprev_ref[...] = k0_ref[...]
        vprev_ref[...] = v0_ref[...]

    jmin = jnp.where(first, first_block_min_key, 0)
    rows = GROUP * BLOCK
    r_idx = lax.broadcasted_iota(jnp.int32, (rows, 2 * BLOCK), 0) % BLOCK
    j_idx = lax.broadcasted_iota(jnp.int32, (rows, 2 * BLOCK), 1)
    band = (j_idx > r_idx) & (j_idx <= r_idx + WINDOW)
    band_first = band & (j_idx >= jmin)
    row_grp = lax.broadcasted_iota(jnp.int32, (rows, 1), 0) // BLOCK
    lane_grp = lax.broadcasted_iota(jnp.int32, (BLOCK, hw), 1) // HEAD_DIM
    grp_mask = [lane_grp == g for g in range(GROUP)]
    grp_mask_bf = [m.astype(BF16) for m in grp_mask]

    for b in range(nblk):
        cur = slice(b * BLOCK, (b + 1) * BLOCK)
        if b == 0:
            kp, vp, valid = kprev_ref[...], vprev_ref[...], band_first
        else:
            prev = slice((b - 1) * BLOCK, b * BLOCK)
            kp, vp, valid = k_ref[prev, :], v_ref[prev, :], band
        kw = jnp.concatenate([kp, k_ref[cur, :]], axis=0)
        vw = jnp.concatenate([vp, v_ref[cur, :]], axis=0)
        qb = q_ref[cur, :]
        outs = []
        for hh in range(n_kv):
            ls = slice(hh * hw, (hh + 1) * hw)
            q_h, kw_h, vw_h = qb[:, ls], kw[:, ls], vw[:, ls]
            qs = jnp.concatenate([q_h * grp_mask_bf[g] for g in range(GROUP)], axis=0)
            s = lax.dot_general(qs, kw_h, (((1,), (1,)), ((), ())), preferred_element_type=F32)
            s = jnp.where(valid, s, NEG_BIG)
            sink = jnp.zeros((rows, 1), F32)
            for g in range(GROUP):
                sink = jnp.where(row_grp == g, sinks_ref[hh * GROUP + g], sink)
            m = jnp.maximum(jnp.max(s, axis=-1, keepdims=True), sink)
            p = jnp.exp(s - m)
            denom = jnp.sum(p, axis=-1, keepdims=True) + jnp.exp(sink - m)
            inv = 1.0 / denom
            pb = p.astype(BF16)
            o_h = jnp.zeros((BLOCK, hw), F32)
            for g in range(GROUP):
                rs = slice(g * BLOCK, (g + 1) * BLOCK)
                o_h = jnp.where(grp_mask[g], _dot(pb[rs], vw_h) * inv[rs], o_h)
            outs.append(o_h)
        o_ref[cur, :] = jnp.concatenate(outs, axis=1).astype(BF16)

    last = slice((nblk - 1) * BLOCK, nblk * BLOCK)
    kprev_ref[...] = k_ref[last, :]
    vprev_ref[...] = v_ref[last, :]
    out_ref[...] = h_ref[...] + _dot(o_ref[...], wo_ref[...])


def _attention(sinks, q, k, v, k0, v0, h, w_o, *, tq, tiles_per_seq, first_block_min_key):
    t, d = h.shape
    tok = pl.BlockSpec((tq, d), lambda i: (i, 0))
    return pl.pallas_call(
        functools.partial(_attn_kernel, tiles_per_seq=tiles_per_seq,
                          first_block_min_key=first_block_min_key),
        grid=(t // tq,),
        in_specs=[
            pl.BlockSpec(memory_space=pltpu.SMEM),
            tok, tok, tok,
            _const_spec((BLOCK, d)), _const_spec((BLOCK, d)),
            tok,
            _const_spec((d, d)),
        ],
        out_specs=tok,
        out_shape=jax.ShapeDtypeStruct((t, d), F32),
        scratch_shapes=[pltpu.VMEM((BLOCK, d), BF16), pltpu.VMEM((BLOCK, d), BF16),
                        pltpu.VMEM((tq, d), BF16)],
        compiler_params=_params(("arbitrary",)),
        name="swa_attention",
    )(sinks, q, k, v, k0, v0, h, w_o)


def _to_row_tiles(dst_ref, x):
    tm = x.shape[0]
    for c in range(x.shape[1] // V7X_LANES):
        dst_ref[pl.ds(c, tm, stride=V7X_SUBLANES), :] = x[:, c * V7X_LANES:(c + 1) * V7X_LANES]


def _from_row_tiles(src_ref, tm):
    n = src_ref.shape[0] // tm
    return jnp.concatenate([src_ref[pl.ds(c, tm, stride=n), :] for c in range(n)], axis=1)


def _router_kernel(h_ref, g_ref, wr_ref, xt_ref, route_ref):
    xn = _rmsnorm(h_ref[...], g_ref[...])
    x_hi = xn.astype(BF16)
    x_lo = (xn - x_hi.astype(F32)).astype(BF16)
    w = wr_ref[...]
    w_hi = w.astype(BF16)
    w_lo = (w - w_hi.astype(F32)).astype(BF16)
    logits = _dot(x_hi, w_hi) + (_dot(x_hi, w_lo) + _dot(x_lo, w_hi))
    lane = lax.broadcasted_iota(jnp.int32, logits.shape, 1)
    neg_inf = jnp.float32(-jnp.inf)
    lg = jnp.where(lane < N_EXPERTS, logits, neg_inf)
    m1 = jnp.max(lg, axis=-1, keepdims=True)
    i1 = jnp.min(jnp.where(lg == m1, lane, V7X_LANES), axis=-1, keepdims=True)
    lg2 = jnp.where(lane == i1, neg_inf, lg)
    m2 = jnp.max(lg2, axis=-1, keepdims=True)
    i2 = jnp.min(jnp.where(lg2 == m2, lane, V7X_LANES), axis=-1, keepdims=True)
    e = jnp.exp(m2 - m1)
    g1 = 1.0 / (1.0 + e)
    g2 = e / (1.0 + e)
    route = jnp.where(lane == 0, i1.astype(F32),
                      jnp.where(lane == 1, i2.astype(F32),
                                jnp.where(lane == 2, g1, jnp.where(lane == 3, g2, 0.0))))
    route_ref[...] = route
    _to_row_tiles(xt_ref, xn)


def _router(h, g, w_router_padded, *, tm):
    t, d = h.shape
    rt = d // V7X_LANES
    return pl.pallas_call(
        _router_kernel,
        grid=(t // tm,),
        in_specs=[pl.BlockSpec((tm, d), lambda i: (i, 0)), _const_spec((1, d)),
                  _const_spec((d, V7X_LANES))],
        out_specs=[pl.BlockSpec((tm * rt, V7X_LANES), lambda i: (i, 0)),
                   pl.BlockSpec((tm, V7X_LANES), lambda i: (i, 0))],
        out_shape=[jax.ShapeDtypeStruct((t * rt, V7X_LANES), F32),
                   jax.ShapeDtypeStruct((t, V7X_LANES), F32)],
        compiler_params=_params(("arbitrary",)),
        name="moe_router",
    )(h, g, w_router_padded)


def _row_copy(src_ref, src_row, dst_ref, dst_row, sem, rt):
    return pltpu.make_async_copy(src_ref.at[pl.ds(pl.multiple_of(src_row * rt, rt), rt), :],
                                 dst_ref.at[pl.ds(pl.multiple_of(dst_row * rt, rt), rt), :], sem)


def _wait_rows(hbm_ref, vmem_ref, sem):
    pltpu.make_async_copy(hbm_ref.at[pl.ds(0, vmem_ref.shape[0]), :], vmem_ref, sem).wait()


def _expert_kernel(w_tile_ref, w_expert_ref, w_first_ref, w_last_ref, w_lo_ref, w_hi_ref, n_work_ref,
                   src0_ref, src_next_ref, dst_prev_ref, dst_cur_ref, xt_hbm, wg_ref, wu_ref, wd_ref, y_hbm,
                   gbuf_ref, xb_ref, acc_ref, stage_ref, gsem, ssem, *, chunk):
    del w_tile_ref, w_expert_ref
    tm, d = xb_ref.shape
    rt = d // V7X_LANES
    w = pl.program_id(0)
    n_work = n_work_ref[0]
    active = w < n_work
    first = active & (w_first_ref[w] == 1)
    last = active & (w_last_ref[w] == 1)
    row_id = lax.broadcasted_iota(jnp.int32, (tm, 1), 0)

    def masked(part):
        return jnp.where((row_id >= w_lo_ref[w]) & (row_id < w_hi_ref[w]), part, 0.0)

    @pl.when(w == 0)
    def _():
        stage_ref[...] = jnp.zeros_like(stage_ref)

        def body(r, c):
            _row_copy(xt_hbm, src0_ref[r], gbuf_ref, r, gsem, rt).start()
            return c
        lax.fori_loop(0, tm, body, 0)

    @pl.when(first)
    def _():
        _wait_rows(xt_hbm, gbuf_ref, gsem)
        xb_ref[...] = _from_row_tiles(gbuf_ref, tm).astype(BF16)

        def side_dmas(i, n):
            half = (n - 1) // 2
            if i < half:
                for r in range(i * tm // half, (i + 1) * tm // half):
                    _row_copy(xt_hbm, src_next_ref[r], gbuf_ref, r, gsem, rt).start()
            elif i < 2 * half:
                j = i - half
                for r in range(j * tm // half, (j + 1) * tm // half):
                    _row_copy(stage_ref, r, y_hbm, dst_prev_ref[r], ssem, rt).start()

        acc_ref[...] = masked(_swiglu_partial(xb_ref[...], wg_ref, wu_ref, wd_ref, chunk, side_dmas))

    @pl.when(active & jnp.logical_not(first))
    def _():
        acc_ref[...] += masked(_swiglu_partial(xb_ref[...], wg_ref, wu_ref, wd_ref, chunk))

    @pl.when(last)
    def _():
        _wait_rows(y_hbm, stage_ref, ssem)
        _to_row_tiles(stage_ref, acc_ref[...])

    @pl.when(w == n_work - 1)
    def _():
        def body(r, c):
            _row_copy(stage_ref, r, y_hbm, dst_cur_ref[r], ssem, rt).start()
            return c
        lax.fori_loop(0, tm, body, 0)
        _wait_rows(y_hbm, stage_ref, ssem)
        _wait_rows(xt_hbm, gbuf_ref, gsem)


def _experts(plan, sorted_a, xt, wg, wu, wd, *, tm, chunk=512):
    n_e, d, f = wg.shape
    rt = d // V7X_LANES
    n_assign = sorted_a.shape[0]
    n_tiles = n_assign // tm
    n_items = n_tiles + n_e - 1
    dst_rows = jnp.concatenate([sorted_a, n_assign + jnp.arange(tm, dtype=jnp.int32)])
    src_rows = jnp.concatenate([sorted_a // 2, jnp.zeros((tm,), jnp.int32)])
    smem_tile = lambda fn: pl.BlockSpec((tm,), fn, memory_space=pltpu.SMEM)
    grid_spec = pltpu.PrefetchScalarGridSpec(
        num_scalar_prefetch=7,
        grid=(n_items,),
        in_specs=[
            smem_tile(lambda w, wt, *_: (0,)),
            smem_tile(lambda w, wt, *_: (wt[w] + 1,)),
            smem_tile(lambda w, wt, *_: (jnp.where(wt[w] == 0, n_tiles, wt[w] - 1),)),
            smem_tile(lambda w, wt, *_: (wt[w],)),
            pl.BlockSpec(memory_space=pl.ANY),
            pl.BlockSpec((None, d, f), lambda w, wt, we, *_: (we[w], 0, 0)),
            pl.BlockSpec((None, d, f), lambda w, wt, we, *_: (we[w], 0, 0)),
            pl.BlockSpec((None, f, d), lambda w, wt, we, *_: (we[w], 0, 0)),
        ],
        out_specs=pl.BlockSpec(memory_space=pl.ANY),
        scratch_shapes=[pltpu.VMEM((tm * rt, V7X_LANES), F32), pltpu.VMEM((tm, d), BF16),
                        pltpu.VMEM((tm, d), F32), pltpu.VMEM((tm * rt, V7X_LANES), F32),
                        pltpu.SemaphoreType.DMA, pltpu.SemaphoreType.DMA],
    )
    return pl.pallas_call(
        functools.partial(_expert_kernel, chunk=chunk),
        grid_spec=grid_spec,
        out_shape=jax.ShapeDtypeStruct(((n_assign + tm) * rt, V7X_LANES), F32),
        compiler_params=_params(("arbitrary",)),
        name="moe_experts",
    )(*plan, src_rows, src_rows, dst_rows, dst_rows, xt, wg, wu, wd)


def _combine_kernel(y_ref, route_ref, h_ref, g_ref, out_ref):
    tm, d = h_ref.shape
    rt = d // V7X_LANES
    route = route_ref[...]
    g1, g2 = route[:, 2:3], route[:, 3:4]
    y1 = jnp.concatenate([y_ref[pl.ds(c, tm, stride=2 * rt), :] for c in range(rt)], axis=1)
    y2 = jnp.concatenate([y_ref[pl.ds(rt + c, tm, stride=2 * rt), :] for c in range(rt)], axis=1)
    out_ref[...] = _rmsnorm(h_ref[...] + g1 * y1 + g2 * y2, g_ref[...])


def _combine(y, route, h, g, *, tm):
    t, d = h.shape
    rt = d // V7X_LANES
    return pl.pallas_call(
        _combine_kernel,
        grid=(t // tm,),
        in_specs=[pl.BlockSpec((tm * 2 * rt, V7X_LANES), lambda i: (i, 0)),
                  pl.BlockSpec((tm, V7X_LANES), lambda i: (i, 0)),
                  pl.BlockSpec((tm, d), lambda i: (i, 0)), _const_spec((1, d))],
        out_specs=pl.BlockSpec((tm, d), lambda i: (i, 0)),
        out_shape=jax.ShapeDtypeStruct((t, d), F32),
        compiler_params=_params(("arbitrary",)),
        name="moe_combine_norm",
    )(y, route, h, g)


def _routing_plan(e1, e2, tm):
    flat_e = jnp.stack([e1, e2], axis=1).reshape(-1)
    n_assign = flat_e.shape[0]
    ids = jnp.arange(n_assign, dtype=jnp.int32)
    _, sorted_a = lax.sort((flat_e, ids), num_keys=1, is_stable=True)
    experts = jnp.arange(N_EXPERTS, dtype=jnp.int32)
    counts = jnp.sum((flat_e[:, None] == experts[None, :]).astype(jnp.int32), axis=0)
    seg_end = jnp.cumsum(counts)
    seg_start = seg_end - counts
    first_tile = seg_start // tm
    n_items = jnp.where(counts > 0, (seg_end - 1) // tm - first_tile + 1, 0)
    item_end = jnp.cumsum(n_items)
    item_start = item_end - n_items
    n_work = item_end[-1]
    max_items = n_assign // tm + N_EXPERTS - 1
    w = jnp.minimum(jnp.arange(max_items, dtype=jnp.int32), n_work - 1)
    w_expert = jnp.sum((w[:, None] >= item_end[None, :]).astype(jnp.int32), axis=1)
    pick = lambda v: jnp.sum(jnp.where(w_expert[:, None] == experts[None, :], v[None, :], 0), axis=1)
    w_tile = pick(first_tile) + (w - pick(item_start))
    w_lo = jnp.maximum(pick(seg_start) - w_tile * tm, 0)
    w_hi = jnp.minimum(pick(seg_end) - w_tile * tm, tm)
    prev_tile = jnp.concatenate([jnp.full((1,), -1, jnp.int32), w_tile[:-1]])
    next_tile = jnp.concatenate([w_tile[1:], jnp.full((1,), -1, jnp.int32)])
    is_last_item = jnp.arange(max_items) == n_work - 1
    w_first = (w_tile != prev_tile).astype(jnp.int32)
    w_last = ((w_tile != next_tile) | is_last_item).astype(jnp.int32)
    i32 = lambda v: v.astype(jnp.int32)
    plan = (i32(w_tile), i32(w_expert), w_first, w_last, i32(w_lo), i32(w_hi), i32(n_work).reshape(1))
    return plan, sorted_a


def _rope_tables(length):
    pos = jnp.arange(length, dtype=F32)
    inv = ROPE_THETA ** (-jnp.arange(0, HEAD_DIM, 2, dtype=F32) / HEAD_DIM)
    ang = pos[:, None] * inv[None, :]
    cos, sin = jnp.cos(ang), jnp.sin(ang)
    reps = V7X_LANES // HEAD_DIM
    cos_l = jnp.tile(jnp.concatenate([cos, cos], axis=1), (1, reps))
    sin_l = jnp.tile(jnp.concatenate([-sin, sin], axis=1), (1, reps))
    return cos_l, sin_l


def kernel(x, meta_tokens, conv_norm, conv_w_in, conv_w, conv_w_out, ffn_norm, ffn_w_gate, ffn_w_up, ffn_w_down, attn_norm, attn_w_qkv, attn_b_qkv, attn_sinks, attn_w_o, moe_norm, moe_w_router, moe_w_gate, moe_w_up, moe_w_down, final_norm):
    bsz, seq, d = x.shape
    n_meta = meta_tokens.shape[0]
    depth = conv_norm.shape[0] + attn_norm.shape[0]
    assert depth == 2 and conv_norm.shape[0] == 1 and attn_norm.shape[0] == 1
    assert n_meta % V7X_SUBLANES == 0 and n_meta <= BLOCK and n_meta >= CONV_WIDTH - 1
    q_dim = d
    kv_dim = (attn_w_qkv.shape[2] - q_dim) // 2
    n_kv = kv_dim // HEAD_DIM
    assert q_dim == n_kv * GROUP * HEAD_DIM

    tm = 512
    assert seq % tm == 0 and tm % BLOCK == 0
    tiles_per_seq = seq // tm
    t = bsz * seq
    row = lambda a: a.reshape(1, -1).astype(F32)

    h = x.reshape(t, d)
    hm = meta_tokens.astype(F32)

    w_in = conv_w_in[0].astype(BF16)
    w_out = conv_w_out[0].astype(BF16)
    zero_carry = jnp.zeros((V7X_SUBLANES, d), F32)
    hm, vtail_m = _conv_mixer(hm, zero_carry, row(conv_norm[0]), w_in, conv_w[0], w_out,
                              tm=n_meta, tiles_per_seq=1)
    h, _ = _conv_mixer(h, vtail_m, row(conv_norm[0]), w_in, conv_w[0], w_out,
                       tm=tm, tiles_per_seq=tiles_per_seq)
    wg, wu, wd = ffn_w_gate[0].astype(BF16), ffn_w_up[0].astype(BF16), ffn_w_down[0].astype(BF16)
    hm, _ = _ffn(hm, row(ffn_norm[0]), wg, wu, wd, tm=n_meta)
    n_e, _, e_ff = moe_w_gate[0].shape
    h, (eg, eu, ed) = _ffn(h, row(ffn_norm[0]), wg, wu, wd, tm=tm,
                           cast=(moe_w_gate[0].reshape(n_e * d, e_ff), moe_w_up[0].reshape(n_e * d, e_ff),
                                 moe_w_down[0].reshape(n_e * e_ff, d)))

    w_qkv, b_qkv = attn_w_qkv[0].astype(BF16), row(attn_b_qkv[0])
    cos_l, sin_l = _rope_tables(n_meta + seq)
    _, km, vm = _qkv_rope(hm, row(attn_norm[0]), w_qkv, b_qkv, cos_l[:n_meta], sin_l[:n_meta],
                          tm=n_meta, tiles_per_seq=1)
    q, k, v = _qkv_rope(h, row(attn_norm[0]), w_qkv, b_qkv, cos_l[n_meta:], sin_l[n_meta:],
                        tm=tm, tiles_per_seq=tiles_per_seq)
    k0 = jnp.zeros((BLOCK, d), BF16).at[BLOCK - n_meta:].set(km)
    v0 = jnp.zeros((BLOCK, d), BF16).at[BLOCK - n_meta:].set(vm)
    h = _attention(attn_sinks[0].astype(F32), q, k, v, k0, v0, h, attn_w_o[0].astype(BF16),
                   tq=tm, tiles_per_seq=tiles_per_seq, first_block_min_key=BLOCK - n_meta)

    w_r = jnp.zeros((d, V7X_LANES), F32).at[:, :N_EXPERTS].set(moe_w_router[0].astype(F32))
    xt, route = _router(h, row(moe_norm[0]), w_r, tm=tm)
    e1 = route[:, 0].astype(jnp.int32)
    e2 = route[:, 1].astype(jnp.int32)
    plan, sorted_a = _routing_plan(e1, e2, tm)
    y = _experts(plan, sorted_a, xt, eg.reshape(n_e, d, e_ff), eu.reshape(n_e, d, e_ff),
                 ed.reshape(n_e, e_ff, d), tm=tm)
    out = _combine(y, route, h, row(final_norm), tm=tm)
    return out.reshape(bsz, seq, d)
```

```python
import functools

import jax
import jax.numpy as jnp
from jax import lax
from jax.experimental import pallas as pl
from jax.experimental.pallas import tpu as pltpu

CONV_WIDTH = 3
HEAD_DIM = 64
GROUP = 4
WINDOW = 128
BLOCK = 128
ROPE_THETA = 10000.0
N_EXPERTS = 8
RMS_EPS = 1e-5
NEG_BIG = -1e30

V7X_LANES = 128
V7X_SUBLANES = 8
VMEM_LIMIT = 56 * 1024 * 1024

BF16 = jnp.bfloat16
F32 = jnp.float32


def _dot(a, b):
    return jnp.dot(a, b, preferred_element_type=F32)


def _rmsnorm(x, g):
    ms = jnp.mean(x * x, axis=-1, keepdims=True)
    return x * lax.rsqrt(ms + RMS_EPS) * g


def _silu(g):
    return g * (1.0 / (1.0 + jnp.exp(-g)))


def _const_spec(shape):
    nd = len(shape)
    return pl.BlockSpec(shape, lambda *_: (0,) * nd, pipeline_mode=pl.Buffered(1))


def _params(sem):
    return pltpu.CompilerParams(dimension_semantics=sem, vmem_limit_bytes=VMEM_LIMIT)


def _slab_spec(a, nt):
    ns = nt
    while a.shape[0] % ns or (a.shape[0] // ns) % (2 * V7X_SUBLANES):
        assert ns % 2 == 0
        ns //= 2
    steps = nt // ns
    return pl.BlockSpec((a.shape[0] // ns, a.shape[1]), lambda i: (i // steps, 0))


def _convert_slabs(cast_in, cast_out):
    for src, dst in zip(cast_in, cast_out):
        dst[...] = src[...].astype(BF16)


def _conv_mixer_kernel(h_ref, carry0_ref, g_ref, win_ref, cw_ref, wout_ref, *rest, tiles_per_seq, n_cast):
    cast_in, (out_ref, vtail_ref), cast_out = rest[:n_cast], rest[n_cast:n_cast + 2], rest[n_cast + 2:-1]
    carry_ref = rest[-1]
    _convert_slabs(cast_in, cast_out)
    d = h_ref.shape[1]
    tm = h_ref.shape[0]

    @pl.when(pl.program_id(0) % tiles_per_seq == 0)
    def _():
        carry_ref[...] = carry0_ref[...]

    h = h_ref[...]
    xn = _rmsnorm(h, g_ref[...]).astype(BF16)
    b_gate = _dot(xn, win_ref[:, 0:d])
    c_gate = _dot(xn, win_ref[:, d:2 * d])
    u = _dot(xn, win_ref[:, 2 * d:3 * d])
    v = c_gate * u
    carry = carry_ref[...]
    row = lax.broadcasted_iota(jnp.int32, (V7X_SUBLANES, d), 0)
    r1 = pltpu.roll(v, 1, axis=0)
    r2 = pltpu.roll(v, 2, axis=0)
    head1 = jnp.where(row < 1, pltpu.roll(carry, 1, axis=0), r1[0:V7X_SUBLANES])
    head2 = jnp.where(row < 2, pltpu.roll(carry, 2, axis=0), r2[0:V7X_SUBLANES])
    v1 = jnp.concatenate([head1, r1[V7X_SUBLANES:]], axis=0)
    v2 = jnp.concatenate([head2, r2[V7X_SUBLANES:]], axis=0)
    conv = cw_ref[0:1, :] * v2 + cw_ref[1:2, :] * v1 + cw_ref[2:3, :] * v
    y = (b_gate * conv).astype(BF16)
    out_ref[...] = h + _dot(y, wout_ref[...])
    tail = v[tm - V7X_SUBLANES:tm]
    carry_ref[...] = tail
    vtail_ref[...] = tail


def _conv_mixer(h, carry0, g, w_in, conv_w, w_out, *, tm, tiles_per_seq, cast=()):
    t, d = h.shape
    nt = t // tm
    outs = pl.pallas_call(
        functools.partial(_conv_mixer_kernel, tiles_per_seq=tiles_per_seq, n_cast=len(cast)),
        grid=(nt,),
        in_specs=[
            pl.BlockSpec((tm, d), lambda i: (i, 0)),
            _const_spec((V7X_SUBLANES, d)),
            _const_spec((1, d)),
            _const_spec((d, 3 * d)),
            _const_spec((CONV_WIDTH, d)),
            _const_spec((d, d)),
        ] + [_slab_spec(a, nt) for a in cast],
        out_specs=[
            pl.BlockSpec((tm, d), lambda i: (i, 0)),
            pl.BlockSpec((V7X_SUBLANES, d), lambda i: (i, 0)),
        ] + [_slab_spec(a, nt) for a in cast],
        out_shape=[
            jax.ShapeDtypeStruct((t, d), F32),
            jax.ShapeDtypeStruct((nt * V7X_SUBLANES, d), F32),
        ] + [jax.ShapeDtypeStruct(a.shape, BF16) for a in cast],
        scratch_shapes=[pltpu.VMEM((V7X_SUBLANES, d), F32)],
        compiler_params=_params(("arbitrary",)),
        name="conv_mixer",
    )(h, carry0, g, w_in, conv_w, w_out, *cast)
    return outs[0], outs[1], outs[2:]


def _ff_chunks(total, chunk):
    out, s = [], 0
    while s < total:
        out.append((s, min(chunk, total - s)))
        s += chunk
    return out


def _swiglu_partial(x, wg_ref, wu_ref, wd_ref, chunk, after_chunk=None):
    acc = None
    pieces = _ff_chunks(wg_ref.shape[1], chunk)
    for i, (s, n) in enumerate(pieces):
        g = _dot(x, wg_ref[:, s:s + n])
        u = _dot(x, wu_ref[:, s:s + n])
        a = (_silu(g) * u).astype(BF16)
        part = _dot(a, wd_ref[s:s + n, :])
        acc = part if acc is None else acc + part
        if after_chunk is not None:
            after_chunk(i, len(pieces))
    return acc


def _ffn_kernel(h_ref, g_ref, wg_ref, wu_ref, wd_ref, *rest, chunk, n_cast):
    cast_in, out_ref, cast_out = rest[:n_cast], rest[n_cast], rest[n_cast + 1:]
    h = h_ref[...]
    xn = _rmsnorm(h, g_ref[...]).astype(BF16)
    out_ref[...] = h + _swiglu_partial(xn, wg_ref, wu_ref, wd_ref, chunk)
    _convert_slabs(cast_in, cast_out)


def _ffn(h, g, wg, wu, wd, *, tm, chunk=512, cast=()):
    t, d = h.shape
    f = wg.shape[1]
    nt = t // tm
    slab = lambda a: _slab_spec(a, nt)
    outs = pl.pallas_call(
        functools.partial(_ffn_kernel, chunk=chunk, n_cast=len(cast)),
        grid=(nt,),
        in_specs=[
            pl.BlockSpec((tm, d), lambda i: (i, 0)),
            _const_spec((1, d)),
            _const_spec((d, f)),
            _const_spec((d, f)),
            _const_spec((f, d)),
        ] + [slab(a) for a in cast],
        out_specs=[pl.BlockSpec((tm, d), lambda i: (i, 0))] + [slab(a) for a in cast],
        out_shape=[jax.ShapeDtypeStruct((t, d), F32)] + [jax.ShapeDtypeStruct(a.shape, BF16) for a in cast],
        compiler_params=_params(("arbitrary",)),
        name="dense_ffn",
    )(h, g, wg, wu, wd, *cast)
    return outs[0], outs[1:]


def _rope(x, cos, sin_signed, first_half):
    outs = []
    for j in range(x.shape[1] // V7X_LANES):
        xj = x[:, j * V7X_LANES:(j + 1) * V7X_LANES]
        partner = jnp.where(first_half, pltpu.roll(xj, V7X_LANES - HEAD_DIM // 2, axis=1),
                            pltpu.roll(xj, HEAD_DIM // 2, axis=1))
        outs.append(xj * cos + partner * sin_signed)
    return jnp.concatenate(outs, axis=1)


def _repeat_heads(x, lane):
    half = V7X_LANES // 2
    assert HEAD_DIM == half and GROUP * HEAD_DIM == 2 * V7X_LANES
    outs = []
    for j in range(x.shape[1] // V7X_LANES):
        xj = x[:, j * V7X_LANES:(j + 1) * V7X_LANES]
        swapped = pltpu.roll(xj, half, axis=1)
        even = jnp.where(lane < half, xj, swapped)
        odd = jnp.where(lane < half, swapped, xj)
        outs += [even, even, odd, odd]
    return jnp.concatenate(outs, axis=1)


def _project_qkv(h, g_ref, w_ref, b_ref, cos, sin_signed, d):
    xn = _rmsnorm(h, g_ref[...]).astype(BF16)
    lane = lax.broadcasted_iota(jnp.int32, cos.shape, 1)
    first_half = (lane % HEAD_DIM) < (HEAD_DIM // 2)
    kvw = (w_ref.shape[1] - d) // 2
    q = _dot(xn, w_ref[:, 0:d]) + b_ref[:, 0:d]
    k = _dot(xn, w_ref[:, d:d + kvw]) + b_ref[:, d:d + kvw]
    v = _dot(xn, w_ref[:, d + kvw:d + 2 * kvw]) + b_ref[:, d + kvw:d + 2 * kvw]
    q = (_rope(q, cos, sin_signed, first_half) * (HEAD_DIM ** -0.5)).astype(BF16)
    k = _repeat_heads(_rope(k, cos, sin_signed, first_half), lane).astype(BF16)
    v = _repeat_heads(v, lane).astype(BF16)
    return q, k, v


def _qkv_kernel(h_ref, g_ref, w_ref, b_ref, cos_ref, sin_ref, q_ref, k_ref, v_ref):
    q_ref[...], k_ref[...], v_ref[...] = _project_qkv(h_ref[...], g_ref, w_ref, b_ref, cos_ref[...],
                                                      sin_ref[...], h_ref.shape[1])


def _qkv_rope(h, g, w_qkv, b_qkv, cos, sin_signed, *, tm, tiles_per_seq):
    t, d = h.shape
    n = w_qkv.shape[1]
    tok = pl.BlockSpec((tm, d), lambda i: (i, 0))
    tab = pl.BlockSpec((tm, V7X_LANES), lambda i: (i % tiles_per_seq, 0))
    return pl.pallas_call(
        _qkv_kernel,
        grid=(t // tm,),
        in_specs=[tok, _const_spec((1, d)), _const_spec((d, n)), _const_spec((1, n)), tab, tab],
        out_specs=[tok, tok, tok],
        out_shape=[jax.ShapeDtypeStruct((t, d), BF16)] * 3,
        compiler_params=_params(("arbitrary",)),
        name="qkv_rope",
    )(h, g, w_qkv, b_qkv, cos, sin_signed)


def _attn_kernel(sinks_ref, q_ref, k_ref, v_ref, k0_ref, v0_ref, h_ref, wo_ref, out_ref,
                 kprev_ref, vprev_ref, o_ref, *, tiles_per_seq, first_block_min_key):
    tq, d = q_ref.shape
    nblk = tq // BLOCK
    n_kv = d // (GROUP * HEAD_DIM)
    hw = GROUP * HEAD_DIM
    first = (pl.program_id(0) % tiles_per_seq) == 0

    @pl.when(first)
    def _():
        kprev_ref[...] = k0_ref[...]
        vprev_ref[...] = v0_ref[...]

    assert WINDOW == BLOCK
    jmin = jnp.where(first, first_block_min_key, 0)
    rows = GROUP * BLOCK
    r_idx = lax.broadcasted_iota(jnp.int32, (rows, BLOCK), 0) % BLOCK
    c_idx = lax.broadcasted_iota(jnp.int32, (rows, BLOCK), 1)
    use_prev = c_idx > r_idx
    padded_key = use_prev & (c_idx < jmin)
    row_grp = lax.broadcasted_iota(jnp.int32, (rows, 1), 0) // BLOCK
    lane_grp = lax.broadcasted_iota(jnp.int32, (BLOCK, hw), 1) // HEAD_DIM
    grp_mask = [lane_grp == g for g in range(GROUP)]
    grp_mask_bf = [m.astype(BF16) for m in grp_mask]

    for b in range(nblk):
        cur = slice(b * BLOCK, (b + 1) * BLOCK)
        if b == 0:
            kp, vp = kprev_ref[...], vprev_ref[...]
        else:
            prev = slice((b - 1) * BLOCK, b * BLOCK)
            kp, vp = k_ref[prev, :], v_ref[prev, :]
        kw = jnp.concatenate([kp, k_ref[cur, :]], axis=0)
        vw = jnp.concatenate([vp, v_ref[cur, :]], axis=0)
        qb = q_ref[cur, :]
        outs = []
        for hh in range(n_kv):
            ls = slice(hh * hw, (hh + 1) * hw)
            q_h, kw_h, vw_h = qb[:, ls], kw[:, ls], vw[:, ls]
            qs = jnp.concatenate([q_h * grp_mask_bf[g] for g in range(GROUP)], axis=0)
            s2 = lax.dot_general(qs, kw_h, (((1,), (1,)), ((), ())), preferred_element_type=F32)
            s = jnp.where(use_prev, s2[:, :BLOCK], s2[:, BLOCK:])
            if b == 0:
                s = jnp.where(padded_key, NEG_BIG, s)
            sink = jnp.zeros((rows, 1), F32)
            for g in range(GROUP):
                sink = jnp.where(row_grp == g, sinks_ref[hh * GROUP + g], sink)
            m = jnp.maximum(jnp.max(s, axis=-1, keepdims=True), sink)
            p = jnp.exp(s - m)
            denom = jnp.sum(p, axis=-1, keepdims=True) + jnp.exp(sink - m)
            pn = p * (1.0 / denom)
            pb = jnp.concatenate([jnp.where(use_prev, pn, 0.0), jnp.where(use_prev, 0.0, pn)],
                                 axis=1).astype(BF16)
            o_h = jnp.zeros((BLOCK, hw), F32)
            for g in range(GROUP):
                rs = slice(g * BLOCK, (g + 1) * BLOCK)
                o_h = jnp.where(grp_mask[g], _dot(pb[rs], vw_h), o_h)
            outs.append(o_h)
        o_ref[cur, :] = jnp.concatenate(outs, axis=1).astype(BF16)

    last = slice((nblk - 1) * BLOCK, nblk * BLOCK)
    kprev_ref[...] = k_ref[last, :]
    vprev_ref[...] = v_ref[last, :]
    out_ref[...] = h_ref[...] + _dot(o_ref[...], wo_ref[...])


def _attention(sinks, q, k, v, k0, v0, h, w_o, *, tq, tiles_per_seq, first_block_min_key):
    t, d = h.shape
    tok = pl.BlockSpec((tq, d), lambda i: (i, 0))
    return pl.pallas_call(
        functools.partial(_attn_kernel, tiles_per_seq=tiles_per_seq,
                          first_block_min_key=first_block_min_key),
        grid=(t // tq,),
        in_specs=[
            pl.BlockSpec(memory_space=pltpu.SMEM),
            tok, tok, tok,
            _const_spec((BLOCK, d)), _const_spec((BLOCK, d)),
            tok,
            _const_spec((d, d)),
        ],
        out_specs=tok,
        out_shape=jax.ShapeDtypeStruct((t, d), F32),
        scratch_shapes=[pltpu.VMEM((BLOCK, d), BF16), pltpu.VMEM((BLOCK, d), BF16),
                        pltpu.VMEM((tq, d), BF16)],
        compiler_params=_params(("arbitrary",)),
        name="swa_attention",
    )(sinks, q, k, v, k0, v0, h, w_o)


def _to_row_tiles(dst_ref, x):
    tm = x.shape[0]
    for c in range(x.shape[1] // V7X_LANES):
        dst_ref[pl.ds(c, tm, stride=V7X_SUBLANES), :] = x[:, c * V7X_LANES:(c + 1) * V7X_LANES]


def _from_row_tiles(src_ref, tm):
    n = src_ref.shape[0] // tm
    return jnp.concatenate([src_ref[pl.ds(c, tm, stride=n), :] for c in range(n)], axis=1)


def _router_kernel(h_ref, g_ref, wr_ref, xt_ref, route_ref):
    xn = _rmsnorm(h_ref[...], g_ref[...])
    x_hi = xn.astype(BF16)
    x_lo = (xn - x_hi.astype(F32)).astype(BF16)
    w = wr_ref[...]
    w_hi = w.astype(BF16)
    w_lo = (w - w_hi.astype(F32)).astype(BF16)
    logits = _dot(x_hi, w_hi) + (_dot(x_hi, w_lo) + _dot(x_lo, w_hi))
    lane = lax.broadcasted_iota(jnp.int32, logits.shape, 1)
    neg_inf = jnp.float32(-jnp.inf)
    lg = jnp.where(lane < N_EXPERTS, logits, neg_inf)
    m1 = jnp.max(lg, axis=-1, keepdims=True)
    i1 = jnp.min(jnp.where(lg == m1, lane, V7X_LANES), axis=-1, keepdims=True)
    lg2 = jnp.where(lane == i1, neg_inf, lg)
    m2 = jnp.max(lg2, axis=-1, keepdims=True)
    i2 = jnp.min(jnp.where(lg2 == m2, lane, V7X_LANES), axis=-1, keepdims=True)
    e = jnp.exp(m2 - m1)
    g1 = 1.0 / (1.0 + e)
    g2 = e / (1.0 + e)
    route = jnp.where(lane == 0, i1.astype(F32),
                      jnp.where(lane == 1, i2.astype(F32),
                                jnp.where(lane == 2, g1, jnp.where(lane == 3, g2, 0.0))))
    route_ref[...] = route
    _to_row_tiles(xt_ref, xn)


def _router(h, g, w_router_padded, *, tm):
    t, d = h.shape
    rt = d // V7X_LANES
    return pl.pallas_call(
        _router_kernel,
        grid=(t // tm,),
        in_specs=[pl.BlockSpec((tm, d), lambda i: (i, 0)), _const_spec((1, d)),
                  _const_spec((d, V7X_LANES))],
        out_specs=[pl.BlockSpec((tm * rt, V7X_LANES), lambda i: (i, 0)),
                   pl.BlockSpec((tm, V7X_LANES), lambda i: (i, 0))],
        out_shape=[jax.ShapeDtypeStruct((t * rt, V7X_LANES), F32),
                   jax.ShapeDtypeStruct((t, V7X_LANES), F32)],
        compiler_params=_params(("arbitrary",)),
        name="moe_router",
    )(h, g, w_router_padded)


def _row_copy(src_ref, src_row, dst_ref, dst_row, sem, rt):
    return pltpu.make_async_copy(src_ref.at[pl.ds(pl.multiple_of(src_row * rt, rt), rt), :],
                                 dst_ref.at[pl.ds(pl.multiple_of(dst_row * rt, rt), rt), :], sem)


def _wait_rows(hbm_ref, vmem_ref, sem):
    pltpu.make_async_copy(hbm_ref.at[pl.ds(0, vmem_ref.shape[0]), :], vmem_ref, sem).wait()


def _expert_kernel(w_tile_ref, w_expert_ref, w_first_ref, w_last_ref, w_lo_ref, w_hi_ref, n_work_ref,
                   src0_ref, src_next_ref, dst_prev_ref, dst_cur_ref, xt_hbm, wg_ref, wu_ref, wd_ref, y_hbm,
                   gbuf_ref, xb_ref, acc_ref, stage_ref, gsem, ssem, *, chunk):
    del w_tile_ref, w_expert_ref
    tm, d = xb_ref.shape
    rt = d // V7X_LANES
    w = pl.program_id(0)
    n_work = n_work_ref[0]
    active = w < n_work
    first = active & (w_first_ref[w] == 1)
    last = active & (w_last_ref[w] == 1)
    row_id = lax.broadcasted_iota(jnp.int32, (tm, 1), 0)

    def masked(part):
        return jnp.where((row_id >= w_lo_ref[w]) & (row_id < w_hi_ref[w]), part, 0.0)

    @pl.when(w == 0)
    def _():
        stage_ref[...] = jnp.zeros_like(stage_ref)

        def body(r, c):
            _row_copy(xt_hbm, src0_ref[r], gbuf_ref, r, gsem, rt).start()
            return c
        lax.fori_loop(0, tm, body, 0)

    @pl.when(first)
    def _():
        _wait_rows(xt_hbm, gbuf_ref, gsem)
        xb_ref[...] = _from_row_tiles(gbuf_ref, tm).astype(BF16)

        def side_dmas(i, n):
            half = (n - 1) // 2
            if i < half:
                for r in range(i * tm // half, (i + 1) * tm // half):
                    _row_copy(xt_hbm, src_next_ref[r], gbuf_ref, r, gsem, rt).start()
            elif i < 2 * half:
                j = i - half
                for r in range(j * tm // half, (j + 1) * tm // half):
                    _row_copy(stage_ref, r, y_hbm, dst_prev_ref[r], ssem, rt).start()

        acc_ref[...] = masked(_swiglu_partial(xb_ref[...], wg_ref, wu_ref, wd_ref, chunk, side_dmas))

    @pl.when(active & jnp.logical_not(first))
    def _():
        acc_ref[...] += masked(_swiglu_partial(xb_ref[...], wg_ref, wu_ref, wd_ref, chunk))

    @pl.when(last)
    def _():
        _wait_rows(y_hbm, stage_ref, ssem)
        _to_row_tiles(stage_ref, acc_ref[...])

    @pl.when(w == n_work - 1)
    def _():
        def body(r, c):
            _row_copy(stage_ref, r, y_hbm, dst_cur_ref[r], ssem, rt).start()
            return c
        lax.fori_loop(0, tm, body, 0)
        _wait_rows(y_hbm, stage_ref, ssem)
        _wait_rows(xt_hbm, gbuf_ref, gsem)


def _experts(plan, sorted_a, xt, wg, wu, wd, *, tm, chunk=512):
    n_e, d, f = wg.shape
    rt = d // V7X_LANES
    n_assign = sorted_a.shape[0]
    n_tiles = n_assign // tm
    n_items = n_tiles + n_e - 1
    dst_rows = jnp.concatenate([sorted_a, n_assign + jnp.arange(tm, dtype=jnp.int32)])
    src_rows = jnp.concatenate([sorted_a // 2, jnp.zeros((tm,), jnp.int32)])
    smem_tile = lambda fn: pl.BlockSpec((tm,), fn, memory_space=pltpu.SMEM)
    grid_spec = pltpu.PrefetchScalarGridSpec(
        num_scalar_prefetch=7,
        grid=(n_items,),
        in_specs=[
            smem_tile(lambda w, wt, *_: (0,)),
            smem_tile(lambda w, wt, *_: (wt[w] + 1,)),
            smem_tile(lambda w, wt, *_: (jnp.where(wt[w] == 0, n_tiles, wt[w] - 1),)),
            smem_tile(lambda w, wt, *_: (wt[w],)),
            pl.BlockSpec(memory_space=pl.ANY),
            pl.BlockSpec((None, d, f), lambda w, wt, we, *_: (we[w], 0, 0)),
            pl.BlockSpec((None, d, f), lambda w, wt, we, *_: (we[w], 0, 0)),
            pl.BlockSpec((None, f, d), lambda w, wt, we, *_: (we[w], 0, 0)),
        ],
        out_specs=pl.BlockSpec(memory_space=pl.ANY),
        scratch_shapes=[pltpu.VMEM((tm * rt, V7X_LANES), F32), pltpu.VMEM((tm, d), BF16),
                        pltpu.VMEM((tm, d), F32), pltpu.VMEM((tm * rt, V7X_LANES), F32),
                        pltpu.SemaphoreType.DMA, pltpu.SemaphoreType.DMA],
    )
    return pl.pallas_call(
        functools.partial(_expert_kernel, chunk=chunk),
        grid_spec=grid_spec,
        out_shape=jax.ShapeDtypeStruct(((n_assign + tm) * rt, V7X_LANES), F32),
        compiler_params=_params(("arbitrary",)),
        name="moe_experts",
    )(*plan, src_rows, src_rows, dst_rows, dst_rows, xt, wg, wu, wd)


def _combine_kernel(y_ref, route_ref, h_ref, g_ref, out_ref):
    tm, d = h_ref.shape
    rt = d // V7X_LANES
    route = route_ref[...]
    g1, g2 = route[:, 2:3], route[:, 3:4]
    y1 = jnp.concatenate([y_ref[pl.ds(c, tm, stride=2 * rt), :] for c in range(rt)], axis=1)
    y2 = jnp.concatenate([y_ref[pl.ds(rt + c, tm, stride=2 * rt), :] for c in range(rt)], axis=1)
    out_ref[...] = _rmsnorm(h_ref[...] + g1 * y1 + g2 * y2, g_ref[...])


def _combine(y, route, h, g, *, tm):
    t, d = h.shape
    rt = d // V7X_LANES
    return pl.pallas_call(
        _combine_kernel,
        grid=(t // tm,),
        in_specs=[pl.BlockSpec((tm * 2 * rt, V7X_LANES), lambda i: (i, 0)),
                  pl.BlockSpec((tm, V7X_LANES), lambda i: (i, 0)),
                  pl.BlockSpec((tm, d), lambda i: (i, 0)), _const_spec((1, d))],
        out_specs=pl.BlockSpec((tm, d), lambda i: (i, 0)),
        out_shape=jax.ShapeDtypeStruct((t, d), F32),
        compiler_params=_params(("arbitrary",)),
        name="moe_combine_norm",
    )(y, route, h, g)


def _routing_plan(e1, e2, tm):
    flat_e = jnp.stack([e1, e2], axis=1).reshape(-1)
    n_assign = flat_e.shape[0]
    ids = jnp.arange(n_assign, dtype=jnp.int32)
    _, sorted_a = lax.sort((flat_e, ids), num_keys=1, is_stable=True)
    experts = jnp.arange(N_EXPERTS, dtype=jnp.int32)
    counts = jnp.sum((flat_e[:, None] == experts[None, :]).astype(jnp.int32), axis=0)
    seg_end = jnp.cumsum(counts)
    seg_start = seg_end - counts
    first_tile = seg_start // tm
    n_items = jnp.where(counts > 0, (seg_end - 1) // tm - first_tile + 1, 0)
    item_end = jnp.cumsum(n_items)
    item_start = item_end - n_items
    n_work = item_end[-1]
    max_items = n_assign // tm + N_EXPERTS - 1
    w = jnp.minimum(jnp.arange(max_items, dtype=jnp.int32), n_work - 1)
    w_expert = jnp.sum((w[:, None] >= item_end[None, :]).astype(jnp.int32), axis=1)
    pick = lambda v: jnp.sum(jnp.where(w_expert[:, None] == experts[None, :], v[None, :], 0), axis=1)
    w_tile = pick(first_tile) + (w - pick(item_start))
    w_lo = jnp.maximum(pick(seg_start) - w_tile * tm, 0)
    w_hi = jnp.minimum(pick(seg_end) - w_tile * tm, tm)
    prev_tile = jnp.concatenate([jnp.full((1,), -1, jnp.int32), w_tile[:-1]])
    next_tile = jnp.concatenate([w_tile[1:], jnp.full((1,), -1, jnp.int32)])
    is_last_item = jnp.arange(max_items) == n_work - 1
    w_first = (w_tile != prev_tile).astype(jnp.int32)
    w_last = ((w_tile != next_tile) | is_last_item).astype(jnp.int32)
    i32 = lambda v: v.astype(jnp.int32)
    plan = (i32(w_tile), i32(w_expert), w_first, w_last, i32(w_lo), i32(w_hi), i32(n_work).reshape(1))
    return plan, sorted_a


def _rope_tables(length):
    pos = jnp.arange(length, dtype=F32)
    inv = ROPE_THETA ** (-jnp.arange(0, HEAD_DIM, 2, dtype=F32) / HEAD_DIM)
    ang = pos[:, None] * inv[None, :]
    cos, sin = jnp.cos(ang), jnp.sin(ang)
    reps = V7X_LANES // HEAD_DIM
    cos_l = jnp.tile(jnp.concatenate([cos, cos], axis=1), (1, reps))
    sin_l = jnp.tile(jnp.concatenate([-sin, sin], axis=1), (1, reps))
    return cos_l, sin_l


def kernel(x, meta_tokens, conv_norm, conv_w_in, conv_w, conv_w_out, ffn_norm, ffn_w_gate, ffn_w_up, ffn_w_down, attn_norm, attn_w_qkv, attn_b_qkv, attn_sinks, attn_w_o, moe_norm, moe_w_router, moe_w_gate, moe_w_up, moe_w_down, final_norm):
    bsz, seq, d = x.shape
    n_meta = meta_tokens.shape[0]
    depth = conv_norm.shape[0] + attn_norm.shape[0]
    assert depth == 2 and conv_norm.shape[0] == 1 and attn_norm.shape[0] == 1
    assert n_meta % V7X_SUBLANES == 0 and n_meta <= BLOCK and n_meta >= CONV_WIDTH - 1
    q_dim = d
    kv_dim = (attn_w_qkv.shape[2] - q_dim) // 2
    n_kv = kv_dim // HEAD_DIM
    assert q_dim == n_kv * GROUP * HEAD_DIM

    tm = 512
    assert seq % tm == 0 and tm % BLOCK == 0
    tiles_per_seq = seq // tm
    t = bsz * seq
    row = lambda a: a.reshape(1, -1).astype(F32)

    h = x.reshape(t, d)
    hm = meta_tokens.astype(F32)

    w_in = conv_w_in[0].astype(BF16)
    w_out = conv_w_out[0].astype(BF16)
    zero_carry = jnp.zeros((V7X_SUBLANES, d), F32)
    hm, vtail_m, _ = _conv_mixer(hm, zero_carry, row(conv_norm[0]), w_in, conv_w[0], w_out,
                                 tm=n_meta, tiles_per_seq=1)
    h, _, (wg, wu, wd, w_qkv, w_o) = _conv_mixer(
        h, vtail_m, row(conv_norm[0]), w_in, conv_w[0], w_out, tm=tm, tiles_per_seq=tiles_per_seq,
        cast=(ffn_w_gate[0], ffn_w_up[0], ffn_w_down[0], attn_w_qkv[0], attn_w_o[0]))
    hm, _ = _ffn(hm, row(ffn_norm[0]), wg, wu, wd, tm=n_meta)
    n_e, _, e_ff = moe_w_gate[0].shape
    h, (eg, eu, ed) = _ffn(h, row(ffn_norm[0]), wg, wu, wd, tm=tm,
                           cast=(moe_w_gate[0].reshape(n_e * d, e_ff), moe_w_up[0].reshape(n_e * d, e_ff),
                                 moe_w_down[0].reshape(n_e * e_ff, d)))

    b_qkv = row(attn_b_qkv[0])
    cos_l, sin_l = _rope_tables(n_meta + seq)
    _, km, vm = _qkv_rope(hm, row(attn_norm[0]), w_qkv, b_qkv, cos_l[:n_meta], sin_l[:n_meta],
                          tm=n_meta, tiles_per_seq=1)
    q, k, v = _qkv_rope(h, row(attn_norm[0]), w_qkv, b_qkv, cos_l[n_meta:], sin_l[n_meta:],
                        tm=tm, tiles_per_seq=tiles_per_seq)
    k0 = jnp.zeros((BLOCK, d), BF16).at[BLOCK - n_meta:].set(km)
    v0 = jnp.zeros((BLOCK, d), BF16).at[BLOCK - n_meta:].set(vm)
    h = _attention(attn_sinks[0].astype(F32), q, k, v, k0, v0, h, w_o,
                   tq=tm, tiles_per_seq=tiles_per_seq, first_block_min_key=BLOCK - n_meta)

    w_r = jnp.zeros((d, V7X_LANES), F32).at[:, :N_EXPERTS].set(moe_w_router[0].astype(F32))
    xt, route = _router(h, row(moe_norm[0]), w_r, tm=tm)
    e1 = route[:, 0].astype(jnp.int32)
    e2 = route[:, 1].astype(jnp.int32)
    plan, sorted_a = _routing_plan(e1, e2, tm)
    y = _experts(plan, sorted_a, xt, eg.reshape(n_e, d, e_ff), eu.reshape(n_e, d, e_ff),
                 ed.reshape(n_e, e_ff, d), tm=tm)
    out = _combine(y, route, h, row(final_norm), tm=tm)
    return out.reshape(bsz, seq, d)
```

```python
import functools

import jax
import jax.numpy as jnp
from jax import lax
from jax.experimental import pallas as pl
from jax.experimental.pallas import tpu as pltpu

CONV_WIDTH = 3
HEAD_DIM = 64
GROUP = 4
WINDOW = 128
BLOCK = 128
ROPE_THETA = 10000.0
N_EXPERTS = 8
RMS_EPS = 1e-5
NEG_BIG = -1e30

V7X_LANES = 128
V7X_SUBLANES = 8
VMEM_LIMIT = 56 * 1024 * 1024

BF16 = jnp.bfloat16
F32 = jnp.float32


def _dot(a, b):
    return jnp.dot(a, b, preferred_element_type=F32)


def _rmsnorm(x, g):
    ms = jnp.mean(x * x, axis=-1, keepdims=True)
    return x * lax.rsqrt(ms + RMS_EPS) * g


def _silu(g):
    return g * (1.0 / (1.0 + jnp.exp(-g)))


def _const_spec(shape):
    nd = len(shape)
    return pl.BlockSpec(shape, lambda *_: (0,) * nd, pipeline_mode=pl.Buffered(1))


def _params(sem):
    return pltpu.CompilerParams(dimension_semantics=sem, vmem_limit_bytes=VMEM_LIMIT)


def _row_pieces(tm, n):
    return [slice(i * (tm // n), (i + 1) * (tm // n)) for i in range(n)]


def _slab_spec(a, nt):
    ns = nt
    while a.shape[0] % ns or (a.shape[0] // ns) % (2 * V7X_SUBLANES):
        assert ns % 2 == 0
        ns //= 2
    steps = nt // ns
    return pl.BlockSpec((a.shape[0] // ns, a.shape[1]), lambda i: (i // steps, 0))


def _convert_slabs(cast_in, cast_out):
    for src, dst in zip(cast_in, cast_out):
        dst[...] = src[...].astype(BF16)


def _conv_mixer_kernel(h_ref, carry0_ref, g_ref, win_ref, cw_ref, wout_ref, *rest, tiles_per_seq, n_cast):
    cast_in, (out_ref, vtail_ref), cast_out = rest[:n_cast], rest[n_cast:n_cast + 2], rest[n_cast + 2:-1]
    carry_ref = rest[-1]
    _convert_slabs(cast_in, cast_out)
    d = h_ref.shape[1]
    tm = h_ref.shape[0]

    @pl.when(pl.program_id(0) % tiles_per_seq == 0)
    def _():
        carry_ref[...] = carry0_ref[...]

    h = h_ref[...]
    xn = _rmsnorm(h, g_ref[...]).astype(BF16)
    b_gate = _dot(xn, win_ref[:, 0:d])
    c_gate = _dot(xn, win_ref[:, d:2 * d])
    u = _dot(xn, win_ref[:, 2 * d:3 * d])
    v = c_gate * u
    carry = carry_ref[...]
    row = lax.broadcasted_iota(jnp.int32, (V7X_SUBLANES, d), 0)
    r1 = pltpu.roll(v, 1, axis=0)
    r2 = pltpu.roll(v, 2, axis=0)
    head1 = jnp.where(row < 1, pltpu.roll(carry, 1, axis=0), r1[0:V7X_SUBLANES])
    head2 = jnp.where(row < 2, pltpu.roll(carry, 2, axis=0), r2[0:V7X_SUBLANES])
    v1 = jnp.concatenate([head1, r1[V7X_SUBLANES:]], axis=0)
    v2 = jnp.concatenate([head2, r2[V7X_SUBLANES:]], axis=0)
    conv = cw_ref[0:1, :] * v2 + cw_ref[1:2, :] * v1 + cw_ref[2:3, :] * v
    y = (b_gate * conv).astype(BF16)
    out_ref[...] = h + _dot(y, wout_ref[...])
    tail = v[tm - V7X_SUBLANES:tm]
    carry_ref[...] = tail
    vtail_ref[...] = tail


def _conv_mixer(h, carry0, g, w_in, conv_w, w_out, *, tm, tiles_per_seq, cast=()):
    t, d = h.shape
    nt = t // tm
    outs = pl.pallas_call(
        functools.partial(_conv_mixer_kernel, tiles_per_seq=tiles_per_seq, n_cast=len(cast)),
        grid=(nt,),
        in_specs=[
            pl.BlockSpec((tm, d), lambda i: (i, 0)),
            _const_spec((V7X_SUBLANES, d)),
            _const_spec((1, d)),
            _const_spec((d, 3 * d)),
            _const_spec((CONV_WIDTH, d)),
            _const_spec((d, d)),
        ] + [_slab_spec(a, nt) for a in cast],
        out_specs=[
            pl.BlockSpec((tm, d), lambda i: (i, 0)),
            pl.BlockSpec((V7X_SUBLANES, d), lambda i: (i, 0)),
        ] + [_slab_spec(a, nt) for a in cast],
        out_shape=[
            jax.ShapeDtypeStruct((t, d), F32),
            jax.ShapeDtypeStruct((nt * V7X_SUBLANES, d), F32),
        ] + [jax.ShapeDtypeStruct(a.shape, BF16) for a in cast],
        scratch_shapes=[pltpu.VMEM((V7X_SUBLANES, d), F32)],
        compiler_params=_params(("arbitrary",)),
        name="conv_mixer",
    )(h, carry0, g, w_in, conv_w, w_out, *cast)
    return outs[0], outs[1], outs[2:]


def _ff_chunks(total, chunk):
    out, s = [], 0
    while s < total:
        out.append((s, min(chunk, total - s)))
        s += chunk
    return out


def _swiglu_partial(x, wg_ref, wu_ref, wd_ref, chunk, after_chunk=None):
    acc = None
    pieces = _ff_chunks(wg_ref.shape[1], chunk)
    for i, (s, n) in enumerate(pieces):
        g = _dot(x, wg_ref[:, s:s + n])
        u = _dot(x, wu_ref[:, s:s + n])
        a = (_silu(g) * u).astype(BF16)
        part = _dot(a, wd_ref[s:s + n, :])
        acc = part if acc is None else acc + part
        if after_chunk is not None:
            after_chunk(i, len(pieces))
    return acc


def _ffn_kernel(h_ref, g_ref, wg_ref, wu_ref, wd_ref, *rest, chunk, n_cast):
    cast_in, out_ref, cast_out = rest[:n_cast], rest[n_cast], rest[n_cast + 1:]
    h = h_ref[...]
    xn = _rmsnorm(h, g_ref[...]).astype(BF16)
    out_ref[...] = h + _swiglu_partial(xn, wg_ref, wu_ref, wd_ref, chunk)
    _convert_slabs(cast_in, cast_out)


def _ffn(h, g, wg, wu, wd, *, tm, chunk=512, cast=()):
    t, d = h.shape
    f = wg.shape[1]
    nt = t // tm
    slab = lambda a: _slab_spec(a, nt)
    outs = pl.pallas_call(
        functools.partial(_ffn_kernel, chunk=chunk, n_cast=len(cast)),
        grid=(nt,),
        in_specs=[
            pl.BlockSpec((tm, d), lambda i: (i, 0)),
            _const_spec((1, d)),
            _const_spec((d, f)),
            _const_spec((d, f)),
            _const_spec((f, d)),
        ] + [slab(a) for a in cast],
        out_specs=[pl.BlockSpec((tm, d), lambda i: (i, 0))] + [slab(a) for a in cast],
        out_shape=[jax.ShapeDtypeStruct((t, d), F32)] + [jax.ShapeDtypeStruct(a.shape, BF16) for a in cast],
        compiler_params=_params(("arbitrary",)),
        name="dense_ffn",
    )(h, g, wg, wu, wd, *cast)
    return outs[0], outs[1:]


def _rope(x, cos, sin_signed, first_half):
    outs = []
    for j in range(x.shape[1] // V7X_LANES):
        xj = x[:, j * V7X_LANES:(j + 1) * V7X_LANES]
        partner = jnp.where(first_half, pltpu.roll(xj, V7X_LANES - HEAD_DIM // 2, axis=1),
                            pltpu.roll(xj, HEAD_DIM // 2, axis=1))
        outs.append(xj * cos + partner * sin_signed)
    return jnp.concatenate(outs, axis=1)


def _repeat_heads(x, lane):
    half = V7X_LANES // 2
    assert HEAD_DIM == half and GROUP * HEAD_DIM == 2 * V7X_LANES
    outs = []
    for j in range(x.shape[1] // V7X_LANES):
        xj = x[:, j * V7X_LANES:(j + 1) * V7X_LANES]
        swapped = pltpu.roll(xj, half, axis=1)
        even = jnp.where(lane < half, xj, swapped)
        odd = jnp.where(lane < half, swapped, xj)
        outs += [even, even, odd, odd]
    return jnp.concatenate(outs, axis=1)


def _project_qkv(h, g_ref, w_ref, b_ref, cos, sin_signed, d):
    xn = _rmsnorm(h, g_ref[...]).astype(BF16)
    lane = lax.broadcasted_iota(jnp.int32, cos.shape, 1)
    first_half = (lane % HEAD_DIM) < (HEAD_DIM // 2)
    kvw = (w_ref.shape[1] - d) // 2
    q = _dot(xn, w_ref[:, 0:d]) + b_ref[:, 0:d]
    k = _dot(xn, w_ref[:, d:d + kvw]) + b_ref[:, d:d + kvw]
    v = _dot(xn, w_ref[:, d + kvw:d + 2 * kvw]) + b_ref[:, d + kvw:d + 2 * kvw]
    q = (_rope(q, cos, sin_signed, first_half) * (HEAD_DIM ** -0.5)).astype(BF16)
    k = _repeat_heads(_rope(k, cos, sin_signed, first_half), lane).astype(BF16)
    v = _repeat_heads(v, lane).astype(BF16)
    return q, k, v


def _qkv_kernel(h_ref, g_ref, w_ref, b_ref, cos_ref, sin_ref, q_ref, k_ref, v_ref):
    q_ref[...], k_ref[...], v_ref[...] = _project_qkv(h_ref[...], g_ref, w_ref, b_ref, cos_ref[...],
                                                      sin_ref[...], h_ref.shape[1])


def _qkv_rope(h, g, w_qkv, b_qkv, cos, sin_signed, *, tm, tiles_per_seq):
    t, d = h.shape
    n = w_qkv.shape[1]
    tok = pl.BlockSpec((tm, d), lambda i: (i, 0))
    tab = pl.BlockSpec((tm, V7X_LANES), lambda i: (i % tiles_per_seq, 0))
    return pl.pallas_call(
        _qkv_kernel,
        grid=(t // tm,),
        in_specs=[tok, _const_spec((1, d)), _const_spec((d, n)), _const_spec((1, n)), tab, tab],
        out_specs=[tok, tok, tok],
        out_shape=[jax.ShapeDtypeStruct((t, d), BF16)] * 3,
        compiler_params=_params(("arbitrary",)),
        name="qkv_rope",
    )(h, g, w_qkv, b_qkv, cos, sin_signed)


def _attn_kernel(sinks_ref, q_ref, k_ref, v_ref, k0_ref, v0_ref, h_ref, wo_ref, out_ref,
                 kprev_ref, vprev_ref, o_ref, *, tiles_per_seq, first_block_min_key):
    tq, d = q_ref.shape
    nblk = tq // BLOCK
    n_kv = d // (GROUP * HEAD_DIM)
    hw = GROUP * HEAD_DIM
    first = (pl.program_id(0) % tiles_per_seq) == 0

    @pl.when(first)
    def _():
        kprev_ref[...] = k0_ref[...]
        vprev_ref[...] = v0_ref[...]

    jmin = jnp.where(first, first_block_min_key, 0)
    rows = GROUP * BLOCK
    r_idx = lax.broadcasted_iota(jnp.int32, (rows, 2 * BLOCK), 0) % BLOCK
    j_idx = lax.broadcasted_iota(jnp.int32, (rows, 2 * BLOCK), 1)
    band = (j_idx > r_idx) & (j_idx <= r_idx + WINDOW)
    band_first = band & (j_idx >= jmin)
    row_grp = lax.broadcasted_iota(jnp.int32, (rows, 1), 0) // BLOCK
    lane_grp = lax.broadcasted_iota(jnp.int32, (BLOCK, hw), 1) // HEAD_DIM
    grp_mask = [lane_grp == g for g in range(GROUP)]
    grp_mask_bf = [m.astype(BF16) for m in grp_mask]

    for b in range(nblk):
        cur = slice(b * BLOCK, (b + 1) * BLOCK)
        if b == 0:
            kp, vp, valid = kprev_ref[...], vprev_ref[...], band_first
        else:
            prev = slice((b - 1) * BLOCK, b * BLOCK)
            kp, vp, valid = k_ref[prev, :], v_ref[prev, :], band
        kw = jnp.concatenate([kp, k_ref[cur, :]], axis=0)
        vw = jnp.concatenate([vp, v_ref[cur, :]], axis=0)
        qb = q_ref[cur, :]
        outs = []
        for hh in range(n_kv):
            ls = slice(hh * hw, (hh + 1) * hw)
            q_h, kw_h, vw_h = qb[:, ls], kw[:, ls], vw[:, ls]
            qs = jnp.concatenate([q_h * grp_mask_bf[g] for g in range(GROUP)], axis=0)
            s = lax.dot_general(qs, kw_h, (((1,), (1,)), ((), ())), preferred_element_type=F32)
            s = jnp.where(valid, s, NEG_BIG)
            sink = jnp.zeros((rows, 1), F32)
            for g in range(GROUP):
                sink = jnp.where(row_grp == g, sinks_ref[hh * GROUP + g], sink)
            m = jnp.maximum(jnp.max(s, axis=-1, keepdims=True), sink)
            p = jnp.exp(s - m)
            denom = jnp.sum(p, axis=-1, keepdims=True) + jnp.exp(sink - m)
            inv = 1.0 / denom
            pb = p.astype(BF16)
            o_h = jnp.zeros((BLOCK, hw), F32)
            for g in range(GROUP):
                rs = slice(g * BLOCK, (g + 1) * BLOCK)
                o_h = jnp.where(grp_mask[g], _dot(pb[rs], vw_h) * inv[rs], o_h)
            outs.append(o_h)
        o_ref[cur, :] = jnp.concatenate(outs, axis=1).astype(BF16)

    last = slice((nblk - 1) * BLOCK, nblk * BLOCK)
    kprev_ref[...] = k_ref[last, :]
    vprev_ref[...] = v_ref[last, :]
    out_ref[...] = h_ref[...] + _dot(o_ref[...], wo_ref[...])


def _attention(sinks, q, k, v, k0, v0, h, w_o, *, tq, tiles_per_seq, first_block_min_key):
    t, d = h.shape
    tok = pl.BlockSpec((tq, d), lambda i: (i, 0))
    return pl.pallas_call(
        functools.partial(_attn_kernel, tiles_per_seq=tiles_per_seq,
                          first_block_min_key=first_block_min_key),
        grid=(t // tq,),
        in_specs=[
            pl.BlockSpec(memory_space=pltpu.SMEM),
            tok, tok, tok,
            _const_spec((BLOCK, d)), _const_spec((BLOCK, d)),
            tok,
            _const_spec((d, d)),
        ],
        out_specs=tok,
        out_shape=jax.ShapeDtypeStruct((t, d), F32),
        scratch_shapes=[pltpu.VMEM((BLOCK, d), BF16), pltpu.VMEM((BLOCK, d), BF16),
                        pltpu.VMEM((tq, d), BF16)],
        compiler_params=_params(("arbitrary",)),
        name="swa_attention",
    )(sinks, q, k, v, k0, v0, h, w_o)


def _to_row_tiles(dst_ref, x):
    tm = x.shape[0]
    for c in range(x.shape[1] // V7X_LANES):
        dst_ref[pl.ds(c, tm, stride=V7X_SUBLANES), :] = x[:, c * V7X_LANES:(c + 1) * V7X_LANES]


def _from_row_tiles(src_ref, tm):
    n = src_ref.shape[0] // tm
    return jnp.concatenate([src_ref[pl.ds(c, tm, stride=n), :] for c in range(n)], axis=1)


def _router_kernel(h_ref, g_ref, wr_ref, xt_ref, route_ref, ids_ref):
    rt = h_ref.shape[1] // V7X_LANES
    w = wr_ref[...]
    w_hi = w.astype(BF16)
    w_lo = (w - w_hi.astype(F32)).astype(BF16)
    for rows in _row_pieces(h_ref.shape[0], 2):
        xn = _rmsnorm(h_ref[rows, :], g_ref[...])
        x_hi = xn.astype(BF16)
        x_lo = (xn - x_hi.astype(F32)).astype(BF16)
        logits = _dot(x_hi, w_hi) + (_dot(x_hi, w_lo) + _dot(x_lo, w_hi))
        lane = lax.broadcasted_iota(jnp.int32, logits.shape, 1)
        neg_inf = jnp.float32(-jnp.inf)
        lg = jnp.where(lane < N_EXPERTS, logits, neg_inf)
        m1 = jnp.max(lg, axis=-1, keepdims=True)
        i1 = jnp.min(jnp.where(lg == m1, lane, V7X_LANES), axis=-1, keepdims=True)
        lg2 = jnp.where(lane == i1, neg_inf, lg)
        m2 = jnp.max(lg2, axis=-1, keepdims=True)
        i2 = jnp.min(jnp.where(lg2 == m2, lane, V7X_LANES), axis=-1, keepdims=True)
        e = jnp.exp(m2 - m1)
        g1 = 1.0 / (1.0 + e)
        g2 = e / (1.0 + e)
        route = jnp.where(lane == 0, i1.astype(F32),
                          jnp.where(lane == 1, i2.astype(F32),
                                    jnp.where(lane == 2, g1, jnp.where(lane == 3, g2, 0.0))))
        route_ref[rows, :] = route
        ids_ref[:, rows] = route.T[0:V7X_SUBLANES, :].astype(jnp.int32)
        _to_row_tiles(xt_ref.at[pl.ds(rows.start * rt, (rows.stop - rows.start) * rt), :], xn)


def _router(h, g, w_router_padded, *, tm):
    t, d = h.shape
    rt = d // V7X_LANES
    return pl.pallas_call(
        _router_kernel,
        grid=(t // tm,),
        in_specs=[pl.BlockSpec((tm, d), lambda i: (i, 0)), _const_spec((1, d)),
                  _const_spec((d, V7X_LANES))],
        out_specs=[pl.BlockSpec((tm * rt, V7X_LANES), lambda i: (i, 0)),
                   pl.BlockSpec((tm, V7X_LANES), lambda i: (i, 0)),
                   pl.BlockSpec((V7X_SUBLANES, tm), lambda i: (i, 0))],
        out_shape=[jax.ShapeDtypeStruct((t * rt, V7X_LANES), F32),
                   jax.ShapeDtypeStruct((t, V7X_LANES), F32),
                   jax.ShapeDtypeStruct((t // tm * V7X_SUBLANES, tm), jnp.int32)],
        compiler_params=_params(("arbitrary",)),
        name="moe_router",
    )(h, g, w_router_padded)


def _row_copy(src_ref, src_row, dst_ref, dst_row, sem, rt):
    return pltpu.make_async_copy(src_ref.at[pl.ds(pl.multiple_of(src_row * rt, rt), rt), :],
                                 dst_ref.at[pl.ds(pl.multiple_of(dst_row * rt, rt), rt), :], sem)


def _wait_rows(hbm_ref, vmem_ref, sem):
    pltpu.make_async_copy(hbm_ref.at[pl.ds(0, vmem_ref.shape[0]), :], vmem_ref, sem).wait()


def _expert_kernel(w_tile_ref, w_expert_ref, w_first_ref, w_last_ref, w_lo_ref, w_hi_ref, n_work_ref,
                   idx_ref, xt_hbm, wg_ref, wu_ref, wd_ref, y_hbm,
                   gbuf_ref, xb_ref, acc_ref, stage_ref, gsem, ssem, *, chunk):
    del w_tile_ref, w_expert_ref
    tm, d = xb_ref.shape
    rt = d // V7X_LANES
    w = pl.program_id(0)
    n_work = n_work_ref[0]
    active = w < n_work
    first = active & (w_first_ref[w] == 1)
    last = active & (w_last_ref[w] == 1)
    row_id = lax.broadcasted_iota(jnp.int32, (tm, 1), 0)

    def masked(part):
        return jnp.where((row_id >= w_lo_ref[w]) & (row_id < w_hi_ref[w]), part, 0.0)

    @pl.when(w == 0)
    def _():
        stage_ref[...] = jnp.zeros_like(stage_ref)

        def body(r, c):
            _row_copy(xt_hbm, idx_ref[3 * tm + r], gbuf_ref, r, gsem, rt).start()
            return c
        lax.fori_loop(0, tm, body, 0)

    @pl.when(first)
    def _():
        _wait_rows(xt_hbm, gbuf_ref, gsem)
        xb_ref[...] = _from_row_tiles(gbuf_ref, tm).astype(BF16)

        def side_dmas(i, n):
            if i == 0:
                for r in range(tm):
                    _row_copy(xt_hbm, idx_ref[r], gbuf_ref, r, gsem, rt).start()
            elif i == n // 2 - 1:
                for r in range(tm):
                    _row_copy(stage_ref, r, y_hbm, idx_ref[tm + r], ssem, rt).start()

        acc_ref[...] = masked(_swiglu_partial(xb_ref[...], wg_ref, wu_ref, wd_ref, chunk, side_dmas))

    @pl.when(active & jnp.logical_not(first))
    def _():
        acc_ref[...] += masked(_swiglu_partial(xb_ref[...], wg_ref, wu_ref, wd_ref, chunk))

    @pl.when(last)
    def _():
        _wait_rows(y_hbm, stage_ref, ssem)
        _to_row_tiles(stage_ref, acc_ref[...])

    @pl.when(w == n_work - 1)
    def _():
        def body(r, c):
            _row_copy(stage_ref, r, y_hbm, idx_ref[2 * tm + r], ssem, rt).start()
            return c
        lax.fori_loop(0, tm, body, 0)
        _wait_rows(y_hbm, stage_ref, ssem)
        _wait_rows(xt_hbm, gbuf_ref, gsem)


def _experts(plan, sorted_a, xt, wg, wu, wd, *, tm, chunk=512):
    n_e, d, f = wg.shape
    rt = d // V7X_LANES
    n_assign = sorted_a.shape[0]
    n_tiles = n_assign // tm
    n_items = n_tiles + n_e - 1
    n_tok = n_assign // 2
    tok, k = sorted_a % n_tok, sorted_a // n_tok
    dst = (2 * tok + k).reshape(n_tiles, tm)
    src = tok.reshape(n_tiles, tm)
    dump = (n_assign + jnp.arange(tm, dtype=jnp.int32)).reshape(1, tm)
    idx = jnp.concatenate([jnp.concatenate([src[1:], jnp.zeros((1, tm), jnp.int32)], axis=0),
                           jnp.concatenate([dump, dst[:-1]], axis=0), dst, src], axis=1).reshape(-1)
    grid_spec = pltpu.PrefetchScalarGridSpec(
        num_scalar_prefetch=7,
        grid=(n_items,),
        in_specs=[
            pl.BlockSpec((4 * tm,), lambda w, wt, *_: (wt[w],), memory_space=pltpu.SMEM),
            pl.BlockSpec(memory_space=pl.ANY),
            pl.BlockSpec((None, d, f), lambda w, wt, we, *_: (we[w], 0, 0)),
            pl.BlockSpec((None, d, f), lambda w, wt, we, *_: (we[w], 0, 0)),
            pl.BlockSpec((None, f, d), lambda w, wt, we, *_: (we[w], 0, 0)),
        ],
        out_specs=pl.BlockSpec(memory_space=pl.ANY),
        scratch_shapes=[pltpu.VMEM((tm * rt, V7X_LANES), F32), pltpu.VMEM((tm, d), BF16),
                        pltpu.VMEM((tm, d), F32), pltpu.VMEM((tm * rt, V7X_LANES), F32),
                        pltpu.SemaphoreType.DMA, pltpu.SemaphoreType.DMA],
    )
    return pl.pallas_call(
        functools.partial(_expert_kernel, chunk=chunk),
        grid_spec=grid_spec,
        out_shape=jax.ShapeDtypeStruct(((n_assign + tm) * rt, V7X_LANES), F32),
        compiler_params=_params(("arbitrary",)),
        name="moe_experts",
    )(*plan, idx, xt, wg, wu, wd)


def _combine_kernel(y_ref, route_ref, h_ref, g_ref, out_ref):
    tm, d = h_ref.shape
    rt = d // V7X_LANES
    route = route_ref[...]
    g1, g2 = route[:, 2:3], route[:, 3:4]
    y1 = jnp.concatenate([y_ref[pl.ds(c, tm, stride=2 * rt), :] for c in range(rt)], axis=1)
    y2 = jnp.concatenate([y_ref[pl.ds(rt + c, tm, stride=2 * rt), :] for c in range(rt)], axis=1)
    out_ref[...] = _rmsnorm(h_ref[...] + g1 * y1 + g2 * y2, g_ref[...])


def _combine(y, route, h, g, *, tm):
    t, d = h.shape
    rt = d // V7X_LANES
    return pl.pallas_call(
        _combine_kernel,
        grid=(t // tm,),
        in_specs=[pl.BlockSpec((tm * 2 * rt, V7X_LANES), lambda i: (i, 0)),
                  pl.BlockSpec((tm, V7X_LANES), lambda i: (i, 0)),
                  pl.BlockSpec((tm, d), lambda i: (i, 0)), _const_spec((1, d))],
        out_specs=pl.BlockSpec((tm, d), lambda i: (i, 0)),
        out_shape=jax.ShapeDtypeStruct((t, d), F32),
        compiler_params=_params(("arbitrary",)),
        name="moe_combine_norm",
    )(y, route, h, g)


def _routing_plan(e1, e2, tm):
    flat_e = jnp.concatenate([e1, e2])
    n_assign = flat_e.shape[0]
    ids = jnp.arange(n_assign, dtype=jnp.int32)
    _, sorted_a = lax.sort((flat_e, ids), num_keys=1, is_stable=True)
    experts = jnp.arange(N_EXPERTS, dtype=jnp.int32)
    dense_e = flat_e.reshape(-1, V7X_LANES)
    counts = jnp.sum((dense_e[None] == experts[:, None, None]).astype(jnp.int32), axis=(1, 2))
    seg_end = jnp.cumsum(counts)
    seg_start = seg_end - counts
    first_tile = seg_start // tm
    n_items = jnp.where(counts > 0, (seg_end - 1) // tm - first_tile + 1, 0)
    item_end = jnp.cumsum(n_items)
    item_start = item_end - n_items
    n_work = item_end[-1]
    max_items = n_assign // tm + N_EXPERTS - 1
    w = jnp.minimum(jnp.arange(max_items, dtype=jnp.int32), n_work - 1)
    w_expert = jnp.sum((w[:, None] >= item_end[None, :]).astype(jnp.int32), axis=1)
    pick = lambda v: jnp.sum(jnp.where(w_expert[:, None] == experts[None, :], v[None, :], 0), axis=1)
    w_tile = pick(first_tile) + (w - pick(item_start))
    w_lo = jnp.maximum(pick(seg_start) - w_tile * tm, 0)
    w_hi = jnp.minimum(pick(seg_end) - w_tile * tm, tm)
    prev_tile = jnp.concatenate([jnp.full((1,), -1, jnp.int32), w_tile[:-1]])
    next_tile = jnp.concatenate([w_tile[1:], jnp.full((1,), -1, jnp.int32)])
    is_last_item = jnp.arange(max_items) == n_work - 1
    w_first = (w_tile != prev_tile).astype(jnp.int32)
    w_last = ((w_tile != next_tile) | is_last_item).astype(jnp.int32)
    i32 = lambda v: v.astype(jnp.int32)
    plan = (i32(w_tile), i32(w_expert), w_first, w_last, i32(w_lo), i32(w_hi), i32(n_work).reshape(1))
    return plan, sorted_a


def _rope_tables(length):
    pos = jnp.arange(length, dtype=F32)
    inv = ROPE_THETA ** (-jnp.arange(0, HEAD_DIM, 2, dtype=F32) / HEAD_DIM)
    ang = pos[:, None] * inv[None, :]
    cos, sin = jnp.cos(ang), jnp.sin(ang)
    reps = V7X_LANES // HEAD_DIM
    cos_l = jnp.tile(jnp.concatenate([cos, cos], axis=1), (1, reps))
    sin_l = jnp.tile(jnp.concatenate([-sin, sin], axis=1), (1, reps))
    return cos_l, sin_l


def kernel(x, meta_tokens, conv_norm, conv_w_in, conv_w, conv_w_out, ffn_norm, ffn_w_gate, ffn_w_up, ffn_w_down, attn_norm, attn_w_qkv, attn_b_qkv, attn_sinks, attn_w_o, moe_norm, moe_w_router, moe_w_gate, moe_w_up, moe_w_down, final_norm):
    bsz, seq, d = x.shape
    n_meta = meta_tokens.shape[0]
    depth = conv_norm.shape[0] + attn_norm.shape[0]
    assert depth == 2 and conv_norm.shape[0] == 1 and attn_norm.shape[0] == 1
    assert n_meta % V7X_SUBLANES == 0 and n_meta <= BLOCK and n_meta >= CONV_WIDTH - 1
    q_dim = d
    kv_dim = (attn_w_qkv.shape[2] - q_dim) // 2
    n_kv = kv_dim // HEAD_DIM
    assert q_dim == n_kv * GROUP * HEAD_DIM

    tm = 512
    assert seq % tm == 0 and tm % BLOCK == 0
    tiles_per_seq = seq // tm
    t = bsz * seq
    row = lambda a: a.reshape(1, -1).astype(F32)

    h = x.reshape(t, d)
    hm = meta_tokens.astype(F32)

    w_in = conv_w_in[0].astype(BF16)
    w_out = conv_w_out[0].astype(BF16)
    zero_carry = jnp.zeros((V7X_SUBLANES, d), F32)
    hm, vtail_m, _ = _conv_mixer(hm, zero_carry, row(conv_norm[0]), w_in, conv_w[0], w_out,
                                 tm=n_meta, tiles_per_seq=1)
    h, _, (wg, wu, wd, w_qkv, w_o) = _conv_mixer(
        h, vtail_m, row(conv_norm[0]), w_in, conv_w[0], w_out, tm=tm, tiles_per_seq=tiles_per_seq,
        cast=(ffn_w_gate[0], ffn_w_up[0], ffn_w_down[0], attn_w_qkv[0], attn_w_o[0]))
    hm, _ = _ffn(hm, row(ffn_norm[0]), wg, wu, wd, tm=n_meta)
    n_e, _, e_ff = moe_w_gate[0].shape
    h, (eg, eu, ed) = _ffn(h, row(ffn_norm[0]), wg, wu, wd, tm=tm,
                           cast=(moe_w_gate[0].reshape(n_e * d, e_ff), moe_w_up[0].reshape(n_e * d, e_ff),
                                 moe_w_down[0].reshape(n_e * e_ff, d)))

    b_qkv = row(attn_b_qkv[0])
    cos_l, sin_l = _rope_tables(n_meta + seq)
    _, km, vm = _qkv_rope(hm, row(attn_norm[0]), w_qkv, b_qkv, cos_l[:n_meta], sin_l[:n_meta],
                          tm=n_meta, tiles_per_seq=1)
    q, k, v = _qkv_rope(h, row(attn_norm[0]), w_qkv, b_qkv, cos_l[n_meta:], sin_l[n_meta:],
                        tm=tm, tiles_per_seq=tiles_per_seq)
    k0 = jnp.zeros((BLOCK, d), BF16).at[BLOCK - n_meta:].set(km)
    v0 = jnp.zeros((BLOCK, d), BF16).at[BLOCK - n_meta:].set(vm)
    h = _attention(attn_sinks[0].astype(F32), q, k, v, k0, v0, h, w_o,
                   tq=tm, tiles_per_seq=tiles_per_seq, first_block_min_key=BLOCK - n_meta)

    w_r = jnp.zeros((d, V7X_LANES), F32).at[:, :N_EXPERTS].set(moe_w_router[0].astype(F32))
    xt, route, ids = _router(h, row(moe_norm[0]), w_r, tm=tm)
    ids = ids.reshape(t // tm, V7X_SUBLANES, tm)
    e1 = ids[:, 0, :].reshape(-1)
    e2 = ids[:, 1, :].reshape(-1)
    plan, sorted_a = _routing_plan(e1, e2, tm)
    y = _experts(plan, sorted_a, xt, eg.reshape(n_e, d, e_ff), eu.reshape(n_e, d, e_ff),
                 ed.reshape(n_e, e_ff, d), tm=tm)
    out = _combine(y, route, h, row(final_norm), tm=tm)
    return out.reshape(bsz, seq, d)
```

```python
import functools

import jax
import jax.numpy as jnp
from jax import lax
from jax.experimental import pallas as pl
from jax.experimental.pallas import tpu as pltpu

CONV_WIDTH = 3
HEAD_DIM = 64
GROUP = 4
WINDOW = 128
BLOCK = 128
ROPE_THETA = 10000.0
N_EXPERTS = 8
RMS_EPS = 1e-5
NEG_BIG = -1e30
LOG2_E = 1.4426950408889634

V7X_LANES = 128
V7X_SUBLANES = 8
VMEM_LIMIT = 56 * 1024 * 1024

BF16 = jnp.bfloat16
F32 = jnp.float32


def _dot(a, b):
    return jnp.dot(a, b, preferred_element_type=F32)


def _rmsnorm(x, g):
    ms = jnp.mean(x * x, axis=-1, keepdims=True)
    return x * lax.rsqrt(ms + RMS_EPS) * g


def _silu(g):
    return g * (1.0 / (1.0 + jnp.exp(-g)))


def _const_spec(shape):
    nd = len(shape)
    return pl.BlockSpec(shape, lambda *_: (0,) * nd, pipeline_mode=pl.Buffered(1))


def _params(sem):
    return pltpu.CompilerParams(dimension_semantics=sem, vmem_limit_bytes=VMEM_LIMIT)


def _row_pieces(tm, n):
    return [slice(i * (tm // n), (i + 1) * (tm // n)) for i in range(n)]


def _slab_spec(a, nt):
    ns = nt
    while a.shape[0] % ns or (a.shape[0] // ns) % (2 * V7X_SUBLANES):
        assert ns % 2 == 0
        ns //= 2
    steps = nt // ns
    return pl.BlockSpec((a.shape[0] // ns, a.shape[1]), lambda i: (i // steps, 0))


def _convert_slabs(cast_in, cast_out):
    for src, dst in zip(cast_in, cast_out):
        dst[...] = src[...].astype(BF16)


def _conv_mixer_kernel(h_ref, carry0_ref, g_ref, win_ref, cw_ref, wout_ref, *rest, tiles_per_seq, n_cast):
    cast_in, (out_ref, vtail_ref), cast_out = rest[:n_cast], rest[n_cast:n_cast + 2], rest[n_cast + 2:-1]
    carry_ref = rest[-1]
    _convert_slabs(cast_in, cast_out)
    d = h_ref.shape[1]
    tm = h_ref.shape[0]

    @pl.when(pl.program_id(0) % tiles_per_seq == 0)
    def _():
        carry_ref[...] = carry0_ref[...]

    h = h_ref[...]
    xn = _rmsnorm(h, g_ref[...]).astype(BF16)
    b_gate = _dot(xn, win_ref[:, 0:d])
    c_gate = _dot(xn, win_ref[:, d:2 * d])
    u = _dot(xn, win_ref[:, 2 * d:3 * d])
    v = c_gate * u
    carry = carry_ref[...]
    row = lax.broadcasted_iota(jnp.int32, (V7X_SUBLANES, d), 0)
    r1 = pltpu.roll(v, 1, axis=0)
    r2 = pltpu.roll(v, 2, axis=0)
    head1 = jnp.where(row < 1, pltpu.roll(carry, 1, axis=0), r1[0:V7X_SUBLANES])
    head2 = jnp.where(row < 2, pltpu.roll(carry, 2, axis=0), r2[0:V7X_SUBLANES])
    v1 = jnp.concatenate([head1, r1[V7X_SUBLANES:]], axis=0)
    v2 = jnp.concatenate([head2, r2[V7X_SUBLANES:]], axis=0)
    conv = cw_ref[0:1, :] * v2 + cw_ref[1:2, :] * v1 + cw_ref[2:3, :] * v
    y = (b_gate * conv).astype(BF16)
    out_ref[...] = h + _dot(y, wout_ref[...])
    tail = v[tm - V7X_SUBLANES:tm]
    carry_ref[...] = tail
    vtail_ref[...] = tail


def _conv_mixer(h, carry0, g, w_in, conv_w, w_out, *, tm, tiles_per_seq, cast=()):
    t, d = h.shape
    nt = t // tm
    outs = pl.pallas_call(
        functools.partial(_conv_mixer_kernel, tiles_per_seq=tiles_per_seq, n_cast=len(cast)),
        grid=(nt,),
        in_specs=[
            pl.BlockSpec((tm, d), lambda i: (i, 0)),
            _const_spec((V7X_SUBLANES, d)),
            _const_spec((1, d)),
            _const_spec((d, 3 * d)),
            _const_spec((CONV_WIDTH, d)),
            _const_spec((d, d)),
        ] + [_slab_spec(a, nt) for a in cast],
        out_specs=[
            pl.BlockSpec((tm, d), lambda i: (i, 0)),
            pl.BlockSpec((V7X_SUBLANES, d), lambda i: (i, 0)),
        ] + [_slab_spec(a, nt) for a in cast],
        out_shape=[
            jax.ShapeDtypeStruct((t, d), F32),
            jax.ShapeDtypeStruct((nt * V7X_SUBLANES, d), F32),
        ] + [jax.ShapeDtypeStruct(a.shape, BF16) for a in cast],
        scratch_shapes=[pltpu.VMEM((V7X_SUBLANES, d), F32)],
        compiler_params=_params(("arbitrary",)),
        name="conv_mixer",
    )(h, carry0, g, w_in, conv_w, w_out, *cast)
    return outs[0], outs[1], outs[2:]


def _ff_chunks(total, chunk):
    out, s = [], 0
    while s < total:
        out.append((s, min(chunk, total - s)))
        s += chunk
    return out


def _swiglu_partial(x, wg_ref, wu_ref, wd_ref, chunk, after_chunk=None):
    acc = None
    pieces = _ff_chunks(wg_ref.shape[1], chunk)
    for i, (s, n) in enumerate(pieces):
        g = _dot(x, wg_ref[:, s:s + n])
        u = _dot(x, wu_ref[:, s:s + n])
        a = (_silu(g) * u).astype(BF16)
        part = _dot(a, wd_ref[s:s + n, :])
        acc = part if acc is None else acc + part
        if after_chunk is not None:
            after_chunk(i, len(pieces))
    return acc


def _ffn_kernel(h_ref, g_ref, wg_ref, wu_ref, wd_ref, *rest, chunk, n_cast):
    cast_in, out_ref, cast_out = rest[:n_cast], rest[n_cast], rest[n_cast + 1:]
    h = h_ref[...]
    xn = _rmsnorm(h, g_ref[...]).astype(BF16)
    out_ref[...] = h + _swiglu_partial(xn, wg_ref, wu_ref, wd_ref, chunk)
    _convert_slabs(cast_in, cast_out)


def _ffn(h, g, wg, wu, wd, *, tm, chunk=512, cast=()):
    t, d = h.shape
    f = wg.shape[1]
    nt = t // tm
    slab = lambda a: _slab_spec(a, nt)
    outs = pl.pallas_call(
        functools.partial(_ffn_kernel, chunk=chunk, n_cast=len(cast)),
        grid=(nt,),
        in_specs=[
            pl.BlockSpec((tm, d), lambda i: (i, 0)),
            _const_spec((1, d)),
            _const_spec((d, f)),
            _const_spec((d, f)),
            _const_spec((f, d)),
        ] + [slab(a) for a in cast],
        out_specs=[pl.BlockSpec((tm, d), lambda i: (i, 0))] + [slab(a) for a in cast],
        out_shape=[jax.ShapeDtypeStruct((t, d), F32)] + [jax.ShapeDtypeStruct(a.shape, BF16) for a in cast],
        compiler_params=_params(("arbitrary",)),
        name="dense_ffn",
    )(h, g, wg, wu, wd, *cast)
    return outs[0], outs[1:]


def _rope(x, cos, sin_signed, first_half):
    outs = []
    for j in range(x.shape[1] // V7X_LANES):
        xj = x[:, j * V7X_LANES:(j + 1) * V7X_LANES]
        partner = jnp.where(first_half, pltpu.roll(xj, V7X_LANES - HEAD_DIM // 2, axis=1),
                            pltpu.roll(xj, HEAD_DIM // 2, axis=1))
        outs.append(xj * cos + partner * sin_signed)
    return jnp.concatenate(outs, axis=1)


def _repeat_heads(x, lane):
    half = V7X_LANES // 2
    assert HEAD_DIM == half and GROUP * HEAD_DIM == 2 * V7X_LANES
    outs = []
    for j in range(x.shape[1] // V7X_LANES):
        xj = x[:, j * V7X_LANES:(j + 1) * V7X_LANES]
        swapped = pltpu.roll(xj, half, axis=1)
        even = jnp.where(lane < half, xj, swapped)
        odd = jnp.where(lane < half, swapped, xj)
        outs += [even, even, odd, odd]
    return jnp.concatenate(outs, axis=1)


def _project_qkv(h, g_ref, w_ref, b_ref, cos, sin_signed, d):
    xn = _rmsnorm(h, g_ref[...]).astype(BF16)
    lane = lax.broadcasted_iota(jnp.int32, cos.shape, 1)
    first_half = (lane % HEAD_DIM) < (HEAD_DIM // 2)
    kvw = (w_ref.shape[1] - d) // 2
    q = _dot(xn, w_ref[:, 0:d]) + b_ref[:, 0:d]
    k = _dot(xn, w_ref[:, d:d + kvw]) + b_ref[:, d:d + kvw]
    v = _dot(xn, w_ref[:, d + kvw:d + 2 * kvw]) + b_ref[:, d + kvw:d + 2 * kvw]
    q = (_rope(q, cos, sin_signed, first_half) * (LOG2_E * HEAD_DIM ** -0.5)).astype(BF16)
    k = _repeat_heads(_rope(k, cos, sin_signed, first_half), lane).astype(BF16)
    v = _repeat_heads(v, lane).astype(BF16)
    return q, k, v


def _qkv_kernel(h_ref, g_ref, w_ref, b_ref, cos_ref, sin_ref, q_ref, k_ref, v_ref):
    q_ref[...], k_ref[...], v_ref[...] = _project_qkv(h_ref[...], g_ref, w_ref, b_ref, cos_ref[...],
                                                      sin_ref[...], h_ref.shape[1])


def _qkv_rope(h, g, w_qkv, b_qkv, cos, sin_signed, *, tm, tiles_per_seq):
    t, d = h.shape
    n = w_qkv.shape[1]
    tok = pl.BlockSpec((tm, d), lambda i: (i, 0))
    tab = pl.BlockSpec((tm, V7X_LANES), lambda i: (i % tiles_per_seq, 0))
    return pl.pallas_call(
        _qkv_kernel,
        grid=(t // tm,),
        in_specs=[tok, _const_spec((1, d)), _const_spec((d, n)), _const_spec((1, n)), tab, tab],
        out_specs=[tok, tok, tok],
        out_shape=[jax.ShapeDtypeStruct((t, d), BF16)] * 3,
        compiler_params=_params(("arbitrary",)),
        name="qkv_rope",
    )(h, g, w_qkv, b_qkv, cos, sin_signed)


def _attn_kernel(sinks_ref, q_ref, k_ref, v_ref, k0_ref, v0_ref, h_ref, wo_ref, out_ref,
                 kprev_ref, vprev_ref, o_ref, *, tiles_per_seq, first_block_min_key):
    tq, d = q_ref.shape
    nblk = tq // BLOCK
    n_kv = d // (GROUP * HEAD_DIM)
    hw = GROUP * HEAD_DIM
    first = (pl.program_id(0) % tiles_per_seq) == 0

    @pl.when(first)
    def _():
        kprev_ref[...] = k0_ref[...]
        vprev_ref[...] = v0_ref[...]

    jmin = jnp.where(first, first_block_min_key, 0)
    rows = GROUP * BLOCK
    r_idx = lax.broadcasted_iota(jnp.int32, (rows, 2 * BLOCK), 0) % BLOCK
    j_idx = lax.broadcasted_iota(jnp.int32, (rows, 2 * BLOCK), 1)
    band = (j_idx > r_idx) & (j_idx <= r_idx + WINDOW)
    band_first = band & (j_idx >= jmin)
    row_grp = lax.broadcasted_iota(jnp.int32, (rows, 1), 0) // BLOCK
    lane_grp = lax.broadcasted_iota(jnp.int32, (BLOCK, hw), 1) // HEAD_DIM
    grp_mask = [lane_grp == g for g in range(GROUP)]
    grp_mask_bf = [m.astype(BF16) for m in grp_mask]

    for b in range(nblk):
        cur = slice(b * BLOCK, (b + 1) * BLOCK)
        if b == 0:
            kp, vp, valid = kprev_ref[...], vprev_ref[...], band_first
        else:
            prev = slice((b - 1) * BLOCK, b * BLOCK)
            kp, vp, valid = k_ref[prev, :], v_ref[prev, :], band
        kw = jnp.concatenate([kp, k_ref[cur, :]], axis=0)
        vw = jnp.concatenate([vp, v_ref[cur, :]], axis=0)
        qb = q_ref[cur, :]
        outs = []
        for hh in range(n_kv):
            ls = slice(hh * hw, (hh + 1) * hw)
            q_h, kw_h, vw_h = qb[:, ls], kw[:, ls], vw[:, ls]
            qs = jnp.concatenate([q_h * grp_mask_bf[g] for g in range(GROUP)], axis=0)
            s = lax.dot_general(qs, kw_h, (((1,), (1,)), ((), ())), preferred_element_type=F32)
            s = jnp.where(valid, s, NEG_BIG)
            sink = jnp.zeros((rows, 1), F32)
            for g in range(GROUP):
                sink = jnp.where(row_grp == g, sinks_ref[hh * GROUP + g] * LOG2_E, sink)
            m = jnp.maximum(jnp.max(s, axis=-1, keepdims=True), sink)
            p = jnp.exp2(s - m)
            denom = jnp.sum(p, axis=-1, keepdims=True) + jnp.exp2(sink - m)
            inv = 1.0 / denom
            pb = p.astype(BF16)
            o_h = jnp.zeros((BLOCK, hw), F32)
            for g in range(GROUP):
                rs = slice(g * BLOCK, (g + 1) * BLOCK)
                o_h = jnp.where(grp_mask[g], _dot(pb[rs], vw_h) * inv[rs], o_h)
            outs.append(o_h)
        o_ref[cur, :] = jnp.concatenate(outs, axis=1).astype(BF16)

    last = slice((nblk - 1) * BLOCK, nblk * BLOCK)
    kprev_ref[...] = k_ref[last, :]
    vprev_ref[...] = v_ref[last, :]
    out_ref[...] = h_ref[...] + _dot(o_ref[...], wo_ref[...])


def _attention(sinks, q, k, v, k0, v0, h, w_o, *, tq, tiles_per_seq, first_block_min_key):
    t, d = h.shape
    tok = pl.BlockSpec((tq, d), lambda i: (i, 0))
    return pl.pallas_call(
        functools.partial(_attn_kernel, tiles_per_seq=tiles_per_seq,
                          first_block_min_key=first_block_min_key),
        grid=(t // tq,),
        in_specs=[
            pl.BlockSpec(memory_space=pltpu.SMEM),
            tok, tok, tok,
            _const_spec((BLOCK, d)), _const_spec((BLOCK, d)),
            tok,
            _const_spec((d, d)),
        ],
        out_specs=tok,
        out_shape=jax.ShapeDtypeStruct((t, d), F32),
        scratch_shapes=[pltpu.VMEM((BLOCK, d), BF16), pltpu.VMEM((BLOCK, d), BF16),
                        pltpu.VMEM((tq, d), BF16)],
        compiler_params=_params(("arbitrary",)),
        name="swa_attention",
    )(sinks, q, k, v, k0, v0, h, w_o)


def _to_row_tiles(dst_ref, x):
    tm = x.shape[0]
    for c in range(x.shape[1] // V7X_LANES):
        dst_ref[pl.ds(c, tm, stride=V7X_SUBLANES), :] = x[:, c * V7X_LANES:(c + 1) * V7X_LANES]


def _from_row_tiles(src_ref, tm):
    n = src_ref.shape[0] // tm
    return jnp.concatenate([src_ref[pl.ds(c, tm, stride=n), :] for c in range(n)], axis=1)


def _router_kernel(h_ref, g_ref, wr_ref, xt_ref, route_ref, ids_ref):
    rt = h_ref.shape[1] // V7X_LANES
    w = wr_ref[...]
    w_hi = w.astype(BF16)
    w_lo = (w - w_hi.astype(F32)).astype(BF16)
    for rows in _row_pieces(h_ref.shape[0], 4):
        xn = _rmsnorm(h_ref[rows, :], g_ref[...])
        x_hi = xn.astype(BF16)
        x_lo = (xn - x_hi.astype(F32)).astype(BF16)
        logits = _dot(x_hi, w_hi) + (_dot(x_hi, w_lo) + _dot(x_lo, w_hi))
        lane = lax.broadcasted_iota(jnp.int32, logits.shape, 1)
        neg_inf = jnp.float32(-jnp.inf)
        lg = jnp.where(lane < N_EXPERTS, logits, neg_inf)
        m1 = jnp.max(lg, axis=-1, keepdims=True)
        i1 = jnp.min(jnp.where(lg == m1, lane, V7X_LANES), axis=-1, keepdims=True)
        lg2 = jnp.where(lane == i1, neg_inf, lg)
        m2 = jnp.max(lg2, axis=-1, keepdims=True)
        i2 = jnp.min(jnp.where(lg2 == m2, lane, V7X_LANES), axis=-1, keepdims=True)
        e = jnp.exp(m2 - m1)
        g1 = 1.0 / (1.0 + e)
        g2 = e / (1.0 + e)
        route = jnp.where(lane == 0, i1.astype(F32),
                          jnp.where(lane == 1, i2.astype(F32),
                                    jnp.where(lane == 2, g1, jnp.where(lane == 3, g2, 0.0))))
        route_ref[rows, :] = route
        ids_ref[:, rows] = route.T[0:V7X_SUBLANES, :].astype(jnp.int32)
        _to_row_tiles(xt_ref.at[pl.ds(rows.start * rt, (rows.stop - rows.start) * rt), :], xn)


def _router(h, g, w_router_padded, *, tm):
    t, d = h.shape
    rt = d // V7X_LANES
    return pl.pallas_call(
        _router_kernel,
        grid=(t // tm,),
        in_specs=[pl.BlockSpec((tm, d), lambda i: (i, 0)), _const_spec((1, d)),
                  _const_spec((d, V7X_LANES))],
        out_specs=[pl.BlockSpec((tm * rt, V7X_LANES), lambda i: (i, 0)),
                   pl.BlockSpec((tm, V7X_LANES), lambda i: (i, 0)),
                   pl.BlockSpec((V7X_SUBLANES, tm), lambda i: (i, 0))],
        out_shape=[jax.ShapeDtypeStruct((t * rt, V7X_LANES), F32),
                   jax.ShapeDtypeStruct((t, V7X_LANES), F32),
                   jax.ShapeDtypeStruct((t // tm * V7X_SUBLANES, tm), jnp.int32)],
        compiler_params=_params(("arbitrary",)),
        name="moe_router",
    )(h, g, w_router_padded)


def _row_copy(src_ref, src_row, dst_ref, dst_row, sem, rt):
    return pltpu.make_async_copy(src_ref.at[pl.ds(pl.multiple_of(src_row * rt, rt), rt), :],
                                 dst_ref.at[pl.ds(pl.multiple_of(dst_row * rt, rt), rt), :], sem)


def _wait_rows(hbm_ref, vmem_ref, sem):
    pltpu.make_async_copy(hbm_ref.at[pl.ds(0, vmem_ref.shape[0]), :], vmem_ref, sem).wait()


def _expert_kernel(w_tile_ref, w_expert_ref, w_first_ref, w_last_ref, w_lo_ref, w_hi_ref, n_work_ref,
                   idx_ref, xt_hbm, wg_ref, wu_ref, wd_ref, y_hbm,
                   gbuf_ref, xb_ref, acc_ref, stage_ref, gsem, ssem, *, chunk):
    del w_tile_ref, w_expert_ref
    tm, d = xb_ref.shape
    rt = d // V7X_LANES
    w = pl.program_id(0)
    n_work = n_work_ref[0]
    active = w < n_work
    first = active & (w_first_ref[w] == 1)
    last = active & (w_last_ref[w] == 1)
    row_id = lax.broadcasted_iota(jnp.int32, (tm, 1), 0)

    def masked(part):
        return jnp.where((row_id >= w_lo_ref[w]) & (row_id < w_hi_ref[w]), part, 0.0)

    @pl.when(w == 0)
    def _():
        stage_ref[...] = jnp.zeros_like(stage_ref)

        def body(r, c):
            _row_copy(xt_hbm, idx_ref[3 * tm + r], gbuf_ref, r, gsem, rt).start()
            return c
        lax.fori_loop(0, tm, body, 0)

    @pl.when(first)
    def _():
        _wait_rows(xt_hbm, gbuf_ref, gsem)
        xb_ref[...] = _from_row_tiles(gbuf_ref, tm).astype(BF16)

    def side_dmas(i, n):
        if i == 0:
            for r in range(tm):
                _row_copy(xt_hbm, idx_ref[r], gbuf_ref, r, gsem, rt).start()
        elif i == n // 2 - 1:
            for r in range(tm):
                _row_copy(stage_ref, r, y_hbm, idx_ref[tm + r], ssem, rt).start()

    @pl.when(first)
    def _():
        acc_ref[...] = masked(_swiglu_partial(xb_ref[...], wg_ref, wu_ref, wd_ref, chunk, side_dmas))

    @pl.when(active & jnp.logical_not(first))
    def _():
        acc_ref[...] += masked(_swiglu_partial(xb_ref[...], wg_ref, wu_ref, wd_ref, chunk))

    @pl.when(last)
    def _():
        _wait_rows(y_hbm, stage_ref, ssem)
        _to_row_tiles(stage_ref, acc_ref[...])

    @pl.when(w == n_work - 1)
    def _():
        def body(r, c):
            _row_copy(stage_ref, r, y_hbm, idx_ref[2 * tm + r], ssem, rt).start()
            return c
        lax.fori_loop(0, tm, body, 0)
        _wait_rows(y_hbm, stage_ref, ssem)
        _wait_rows(xt_hbm, gbuf_ref, gsem)


def _experts(plan, sorted_a, xt, wg, wu, wd, *, tm, chunk=512):
    n_e, d, f = wg.shape
    rt = d // V7X_LANES
    n_assign = sorted_a.shape[0]
    n_tiles = n_assign // tm
    n_items = n_tiles + n_e - 1
    n_tok = n_assign // 2
    tok, k = sorted_a % n_tok, sorted_a // n_tok
    dst = (2 * tok + k).reshape(n_tiles, tm)
    src = tok.reshape(n_tiles, tm)
    dump = (n_assign + jnp.arange(tm, dtype=jnp.int32)).reshape(1, tm)
    idx = jnp.concatenate([jnp.concatenate([src[1:], jnp.zeros((1, tm), jnp.int32)], axis=0),
                           jnp.concatenate([dump, dst[:-1]], axis=0), dst, src], axis=1).reshape(-1)
    grid_spec = pltpu.PrefetchScalarGridSpec(
        num_scalar_prefetch=7,
        grid=(n_items,),
        in_specs=[
            pl.BlockSpec((4 * tm,), lambda w, wt, *_: (wt[w],), memory_space=pltpu.SMEM),
            pl.BlockSpec(memory_space=pl.ANY),
            pl.BlockSpec((None, d, f), lambda w, wt, we, *_: (we[w], 0, 0)),
            pl.BlockSpec((None, d, f), lambda w, wt, we, *_: (we[w], 0, 0)),
            pl.BlockSpec((None, f, d), lambda w, wt, we, *_: (we[w], 0, 0)),
        ],
        out_specs=pl.BlockSpec(memory_space=pl.ANY),
        scratch_shapes=[pltpu.VMEM((tm * rt, V7X_LANES), F32), pltpu.VMEM((tm, d), BF16),
                        pltpu.VMEM((tm, d), F32), pltpu.VMEM((tm * rt, V7X_LANES), F32),
                        pltpu.SemaphoreType.DMA, pltpu.SemaphoreType.DMA],
    )
    return pl.pallas_call(
        functools.partial(_expert_kernel, chunk=chunk),
        grid_spec=grid_spec,
        out_shape=jax.ShapeDtypeStruct(((n_assign + tm) * rt, V7X_LANES), F32),
        compiler_params=_params(("arbitrary",)),
        name="moe_experts",
    )(*plan, idx, xt, wg, wu, wd)


def _combine_kernel(y_ref, route_ref, h_ref, g_ref, out_ref):
    tm, d = h_ref.shape
    rt = d // V7X_LANES
    route = route_ref[...]
    g1, g2 = route[:, 2:3], route[:, 3:4]
    y1 = jnp.concatenate([y_ref[pl.ds(c, tm, stride=2 * rt), :] for c in range(rt)], axis=1)
    y2 = jnp.concatenate([y_ref[pl.ds(rt + c, tm, stride=2 * rt), :] for c in range(rt)], axis=1)
    out_ref[...] = _rmsnorm(h_ref[...] + g1 * y1 + g2 * y2, g_ref[...])


def _combine(y, route, h, g, *, tm):
    t, d = h.shape
    rt = d // V7X_LANES
    return pl.pallas_call(
        _combine_kernel,
        grid=(t // tm,),
        in_specs=[pl.BlockSpec((tm * 2 * rt, V7X_LANES), lambda i: (i, 0)),
                  pl.BlockSpec((tm, V7X_LANES), lambda i: (i, 0)),
                  pl.BlockSpec((tm, d), lambda i: (i, 0)), _const_spec((1, d))],
        out_specs=pl.BlockSpec((tm, d), lambda i: (i, 0)),
        out_shape=jax.ShapeDtypeStruct((t, d), F32),
        compiler_params=_params(("arbitrary",)),
        name="moe_combine_norm",
    )(y, route, h, g)


def _routing_plan(e1, e2, tm):
    flat_e = jnp.concatenate([e1, e2])
    n_assign = flat_e.shape[0]
    ids = jnp.arange(n_assign, dtype=jnp.int32)
    _, sorted_a = lax.sort((flat_e, ids), num_keys=1, is_stable=True)
    experts = jnp.arange(N_EXPERTS, dtype=jnp.int32)
    dense_e = flat_e.reshape(-1, V7X_LANES)
    counts = jnp.sum((dense_e[None] == experts[:, None, None]).astype(jnp.int32), axis=(1, 2))
    seg_end = jnp.cumsum(counts)
    seg_start = seg_end - counts
    first_tile = seg_start // tm
    n_items = jnp.where(counts > 0, (seg_end - 1) // tm - first_tile + 1, 0)
    item_end = jnp.cumsum(n_items)
    item_start = item_end - n_items
    n_work = item_end[-1]
    max_items = n_assign // tm + N_EXPERTS - 1
    w = jnp.minimum(jnp.arange(max_items, dtype=jnp.int32), n_work - 1)
    w_expert = jnp.sum((w[:, None] >= item_end[None, :]).astype(jnp.int32), axis=1)
    pick = lambda v: jnp.sum(jnp.where(w_expert[:, None] == experts[None, :], v[None, :], 0), axis=1)
    w_tile = pick(first_tile) + (w - pick(item_start))
    w_lo = jnp.maximum(pick(seg_start) - w_tile * tm, 0)
    w_hi = jnp.minimum(pick(seg_end) - w_tile * tm, tm)
    prev_tile = jnp.concatenate([jnp.full((1,), -1, jnp.int32), w_tile[:-1]])
    next_tile = jnp.concatenate([w_tile[1:], jnp.full((1,), -1, jnp.int32)])
    is_last_item = jnp.arange(max_items) == n_work - 1
    w_first = (w_tile != prev_tile).astype(jnp.int32)
    w_last = ((w_tile != next_tile) | is_last_item).astype(jnp.int32)
    i32 = lambda v: v.astype(jnp.int32)
    plan = (i32(w_tile), i32(w_expert), w_first, w_last, i32(w_lo), i32(w_hi), i32(n_work).reshape(1))
    return plan, sorted_a


def _rope_tables(length):
    pos = jnp.arange(length, dtype=F32)
    inv = ROPE_THETA ** (-jnp.arange(0, HEAD_DIM, 2, dtype=F32) / HEAD_DIM)
    ang = pos[:, None] * inv[None, :]
    cos, sin = jnp.cos(ang), jnp.sin(ang)
    reps = V7X_LANES // HEAD_DIM
    cos_l = jnp.tile(jnp.concatenate([cos, cos], axis=1), (1, reps))
    sin_l = jnp.tile(jnp.concatenate([-sin, sin], axis=1), (1, reps))
    return cos_l, sin_l


def _tile_rows(seq):
    narrow, wide = 4 * BLOCK, 8 * BLOCK
    assert seq % wide == 0
    return narrow, wide


def kernel(x, meta_tokens, conv_norm, conv_w_in, conv_w, conv_w_out, ffn_norm, ffn_w_gate, ffn_w_up, ffn_w_down, attn_norm, attn_w_qkv, attn_b_qkv, attn_sinks, attn_w_o, moe_norm, moe_w_router, moe_w_gate, moe_w_up, moe_w_down, final_norm):
    bsz, seq, d = x.shape
    n_meta = meta_tokens.shape[0]
    depth = conv_norm.shape[0] + attn_norm.shape[0]
    assert depth == 2 and conv_norm.shape[0] == 1 and attn_norm.shape[0] == 1
    assert n_meta % V7X_SUBLANES == 0 and n_meta <= BLOCK and n_meta >= CONV_WIDTH - 1
    q_dim = d
    kv_dim = (attn_w_qkv.shape[2] - q_dim) // 2
    n_kv = kv_dim // HEAD_DIM
    assert q_dim == n_kv * GROUP * HEAD_DIM

    tm, tw = _tile_rows(seq)
    wide_tiles_per_seq = seq // tw
    t = bsz * seq
    row = lambda a: a.reshape(1, -1).astype(F32)

    h = x.reshape(t, d)
    hm = meta_tokens.astype(F32)

    w_in = conv_w_in[0].astype(BF16)
    w_out = conv_w_out[0].astype(BF16)
    zero_carry = jnp.zeros((V7X_SUBLANES, d), F32)
    hm, vtail_m, _ = _conv_mixer(hm, zero_carry, row(conv_norm[0]), w_in, conv_w[0], w_out,
                                 tm=n_meta, tiles_per_seq=1)
    h, _, (wg, wu, wd, w_qkv, w_o) = _conv_mixer(
        h, vtail_m, row(conv_norm[0]), w_in, conv_w[0], w_out, tm=tw, tiles_per_seq=wide_tiles_per_seq,
        cast=(ffn_w_gate[0], ffn_w_up[0], ffn_w_down[0], attn_w_qkv[0], attn_w_o[0]))
    hm, _ = _ffn(hm, row(ffn_norm[0]), wg, wu, wd, tm=n_meta)
    n_e, _, e_ff = moe_w_gate[0].shape
    h, (eg, eu, ed) = _ffn(h, row(ffn_norm[0]), wg, wu, wd, tm=tm,
                           cast=(moe_w_gate[0].reshape(n_e * d, e_ff), moe_w_up[0].reshape(n_e * d, e_ff),
                                 moe_w_down[0].reshape(n_e * e_ff, d)))

    b_qkv = row(attn_b_qkv[0])
    cos_l, sin_l = _rope_tables(n_meta + seq)
    _, km, vm = _qkv_rope(hm, row(attn_norm[0]), w_qkv, b_qkv, cos_l[:n_meta], sin_l[:n_meta],
                          tm=n_meta, tiles_per_seq=1)
    q, k, v = _qkv_rope(h, row(attn_norm[0]), w_qkv, b_qkv, cos_l[n_meta:], sin_l[n_meta:],
                        tm=tw, tiles_per_seq=wide_tiles_per_seq)
    k0 = jnp.zeros((BLOCK, d), BF16).at[BLOCK - n_meta:].set(km)
    v0 = jnp.zeros((BLOCK, d), BF16).at[BLOCK - n_meta:].set(vm)
    h = _attention(attn_sinks[0].astype(F32), q, k, v, k0, v0, h, w_o,
                   tq=tw, tiles_per_seq=wide_tiles_per_seq, first_block_min_key=BLOCK - n_meta)

    w_r = jnp.zeros((d, V7X_LANES), F32).at[:, :N_EXPERTS].set(moe_w_router[0].astype(F32))
    xt, route, ids = _router(h, row(moe_norm[0]), w_r, tm=tw)
    ids = ids.reshape(t // tw, V7X_SUBLANES, tw)
    e1 = ids[:, 0, :].reshape(-1)
    e2 = ids[:, 1, :].reshape(-1)
    plan, sorted_a = _routing_plan(e1, e2, tm)
    y = _experts(plan, sorted_a, xt, eg.reshape(n_e, d, e_ff), eu.reshape(n_e, d, e_ff),
                 ed.reshape(n_e, e_ff, d), tm=tm)
    out = _combine(y, route, h, row(final_norm), tm=tw)
    return out.reshape(bsz, seq, d)
```

```python
import functools

import jax
import jax.numpy as jnp
from jax import lax
from jax.experimental import pallas as pl
from jax.experimental.pallas import tpu as pltpu

CONV_WIDTH = 3
HEAD_DIM = 64
GROUP = 4
WINDOW = 128
BLOCK = 128
ROPE_THETA = 10000.0
N_EXPERTS = 8
RMS_EPS = 1e-5
NEG_BIG = -1e30
LOG2_E = 1.4426950408889634

V7X_LANES = 128
V7X_SUBLANES = 8
VMEM_LIMIT = 56 * 1024 * 1024

BF16 = jnp.bfloat16
F32 = jnp.float32


def _dot(a, b):
    return jnp.dot(a, b, preferred_element_type=F32)


def _rmsnorm(x, g):
    ms = jnp.mean(x * x, axis=-1, keepdims=True)
    return x * lax.rsqrt(ms + RMS_EPS) * g


def _silu(g):
    return g * (1.0 / (1.0 + jnp.exp(-g)))


def _const_spec(shape):
    nd = len(shape)
    return pl.BlockSpec(shape, lambda *_: (0,) * nd, pipeline_mode=pl.Buffered(1))


def _params(sem):
    return pltpu.CompilerParams(dimension_semantics=sem, vmem_limit_bytes=VMEM_LIMIT)


def _row_pieces(tm, n):
    return [slice(i * (tm // n), (i + 1) * (tm // n)) for i in range(n)]


def _slab_spec(a, nt):
    ns = nt
    while a.shape[0] % ns or (a.shape[0] // ns) % (2 * V7X_SUBLANES):
        assert ns % 2 == 0
        ns //= 2
    steps = nt // ns
    return pl.BlockSpec((a.shape[0] // ns, a.shape[1]), lambda i: (i // steps, 0))


def _convert_slabs(cast_in, cast_out):
    for src, dst in zip(cast_in, cast_out):
        dst[...] = src[...].astype(BF16)


def _conv_mixer_kernel(h_ref, carry0_ref, g_ref, win_ref, cw_ref, wout_ref, *rest, tiles_per_seq, n_cast):
    cast_in, (out_ref, vtail_ref), cast_out = rest[:n_cast], rest[n_cast:n_cast + 2], rest[n_cast + 2:-1]
    carry_ref = rest[-1]
    _convert_slabs(cast_in, cast_out)
    d = h_ref.shape[1]
    tm = h_ref.shape[0]

    @pl.when(pl.program_id(0) % tiles_per_seq == 0)
    def _():
        carry_ref[...] = carry0_ref[...]

    h = h_ref[...]
    xn = _rmsnorm(h, g_ref[...]).astype(BF16)
    b_gate = _dot(xn, win_ref[:, 0:d])
    c_gate = _dot(xn, win_ref[:, d:2 * d])
    u = _dot(xn, win_ref[:, 2 * d:3 * d])
    v = c_gate * u
    carry = carry_ref[...]
    row = lax.broadcasted_iota(jnp.int32, (V7X_SUBLANES, d), 0)
    r1 = pltpu.roll(v, 1, axis=0)
    r2 = pltpu.roll(v, 2, axis=0)
    head1 = jnp.where(row < 1, pltpu.roll(carry, 1, axis=0), r1[0:V7X_SUBLANES])
    head2 = jnp.where(row < 2, pltpu.roll(carry, 2, axis=0), r2[0:V7X_SUBLANES])
    v1 = jnp.concatenate([head1, r1[V7X_SUBLANES:]], axis=0)
    v2 = jnp.concatenate([head2, r2[V7X_SUBLANES:]], axis=0)
    conv = cw_ref[0:1, :] * v2 + cw_ref[1:2, :] * v1 + cw_ref[2:3, :] * v
    y = (b_gate * conv).astype(BF16)
    out_ref[...] = h + _dot(y, wout_ref[...])
    tail = v[tm - V7X_SUBLANES:tm]
    carry_ref[...] = tail
    vtail_ref[...] = tail


def _conv_mixer(h, carry0, g, w_in, conv_w, w_out, *, tm, tiles_per_seq, cast=()):
    t, d = h.shape
    nt = t // tm
    outs = pl.pallas_call(
        functools.partial(_conv_mixer_kernel, tiles_per_seq=tiles_per_seq, n_cast=len(cast)),
        grid=(nt,),
        in_specs=[
            pl.BlockSpec((tm, d), lambda i: (i, 0)),
            _const_spec((V7X_SUBLANES, d)),
            _const_spec((1, d)),
            _const_spec((d, 3 * d)),
            _const_spec((CONV_WIDTH, d)),
            _const_spec((d, d)),
        ] + [_slab_spec(a, nt) for a in cast],
        out_specs=[
            pl.BlockSpec((tm, d), lambda i: (i, 0)),
            pl.BlockSpec((V7X_SUBLANES, d), lambda i: (i, 0)),
        ] + [_slab_spec(a, nt) for a in cast],
        out_shape=[
            jax.ShapeDtypeStruct((t, d), F32),
            jax.ShapeDtypeStruct((nt * V7X_SUBLANES, d), F32),
        ] + [jax.ShapeDtypeStruct(a.shape, BF16) for a in cast],
        scratch_shapes=[pltpu.VMEM((V7X_SUBLANES, d), F32)],
        compiler_params=_params(("arbitrary",)),
        name="conv_mixer",
    )(h, carry0, g, w_in, conv_w, w_out, *cast)
    return outs[0], outs[1], outs[2:]


def _ff_chunks(total, chunk):
    out, s = [], 0
    while s < total:
        out.append((s, min(chunk, total - s)))
        s += chunk
    return out


def _swiglu_partial(x, wg_ref, wu_ref, wd_ref, chunk, after_chunk=None):
    acc = None
    pieces = _ff_chunks(wg_ref.shape[1], chunk)
    for i, (s, n) in enumerate(pieces):
        g = _dot(x, wg_ref[:, s:s + n])
        u = _dot(x, wu_ref[:, s:s + n])
        a = (_silu(g) * u).astype(BF16)
        part = _dot(a, wd_ref[s:s + n, :])
        acc = part if acc is None else acc + part
        if after_chunk is not None:
            after_chunk(i, len(pieces))
    return acc


def _ffn_kernel(h_ref, g_ref, wg_ref, wu_ref, wd_ref, *rest, chunk, n_cast):
    cast_in, out_ref, cast_out = rest[:n_cast], rest[n_cast], rest[n_cast + 1:]
    h = h_ref[...]
    xn = _rmsnorm(h, g_ref[...]).astype(BF16)
    out_ref[...] = h + _swiglu_partial(xn, wg_ref, wu_ref, wd_ref, chunk)
    _convert_slabs(cast_in, cast_out)


def _ffn(h, g, wg, wu, wd, *, tm, chunk=512, cast=()):
    t, d = h.shape
    f = wg.shape[1]
    nt = t // tm
    slab = lambda a: _slab_spec(a, nt)
    outs = pl.pallas_call(
        functools.partial(_ffn_kernel, chunk=chunk, n_cast=len(cast)),
        grid=(nt,),
        in_specs=[
            pl.BlockSpec((tm, d), lambda i: (i, 0)),
            _const_spec((1, d)),
            _const_spec((d, f)),
            _const_spec((d, f)),
            _const_spec((f, d)),
        ] + [slab(a) for a in cast],
        out_specs=[pl.BlockSpec((tm, d), lambda i: (i, 0))] + [slab(a) for a in cast],
        out_shape=[jax.ShapeDtypeStruct((t, d), F32)] + [jax.ShapeDtypeStruct(a.shape, BF16) for a in cast],
        compiler_params=_params(("arbitrary",)),
        name="dense_ffn",
    )(h, g, wg, wu, wd, *cast)
    return outs[0], outs[1:]


def _rope(x, cos, sin_signed, first_half):
    outs = []
    for j in range(x.shape[1] // V7X_LANES):
        xj = x[:, j * V7X_LANES:(j + 1) * V7X_LANES]
        partner = jnp.where(first_half, pltpu.roll(xj, V7X_LANES - HEAD_DIM // 2, axis=1),
                            pltpu.roll(xj, HEAD_DIM // 2, axis=1))
        outs.append(xj * cos + partner * sin_signed)
    return jnp.concatenate(outs, axis=1)


def _repeat_heads(x, lane):
    half = V7X_LANES // 2
    assert HEAD_DIM == half and GROUP * HEAD_DIM == 2 * V7X_LANES
    outs = []
    for j in range(x.shape[1] // V7X_LANES):
        xj = x[:, j * V7X_LANES:(j + 1) * V7X_LANES]
        swapped = pltpu.roll(xj, half, axis=1)
        even = jnp.where(lane < half, xj, swapped)
        odd = jnp.where(lane < half, swapped, xj)
        outs += [even, even, odd, odd]
    return jnp.concatenate(outs, axis=1)


def _project_qkv(h, g_ref, w_ref, b_ref, cos, sin_signed, d):
    xn = _rmsnorm(h, g_ref[...]).astype(BF16)
    lane = lax.broadcasted_iota(jnp.int32, cos.shape, 1)
    first_half = (lane % HEAD_DIM) < (HEAD_DIM // 2)
    kvw = (w_ref.shape[1] - d) // 2
    q = _dot(xn, w_ref[:, 0:d]) + b_ref[:, 0:d]
    k = _dot(xn, w_ref[:, d:d + kvw]) + b_ref[:, d:d + kvw]
    v = _dot(xn, w_ref[:, d + kvw:d + 2 * kvw]) + b_ref[:, d + kvw:d + 2 * kvw]
    q = (_rope(q, cos, sin_signed, first_half) * (LOG2_E * HEAD_DIM ** -0.5)).astype(BF16)
    k = _repeat_heads(_rope(k, cos, sin_signed, first_half), lane).astype(BF16)
    v = _repeat_heads(v, lane).astype(BF16)
    return q, k, v


def _qkv_kernel(h_ref, g_ref, w_ref, b_ref, cos_ref, sin_ref, q_ref, k_ref, v_ref):
    q_ref[...], k_ref[...], v_ref[...] = _project_qkv(h_ref[...], g_ref, w_ref, b_ref, cos_ref[...],
                                                      sin_ref[...], h_ref.shape[1])


def _qkv_rope(h, g, w_qkv, b_qkv, cos, sin_signed, *, tm, tiles_per_seq):
    t, d = h.shape
    n = w_qkv.shape[1]
    tok = pl.BlockSpec((tm, d), lambda i: (i, 0))
    tab = pl.BlockSpec((tm, V7X_LANES), lambda i: (i % tiles_per_seq, 0))
    return pl.pallas_call(
        _qkv_kernel,
        grid=(t // tm,),
        in_specs=[tok, _const_spec((1, d)), _const_spec((d, n)), _const_spec((1, n)), tab, tab],
        out_specs=[tok, tok, tok],
        out_shape=[jax.ShapeDtypeStruct((t, d), BF16)] * 3,
        compiler_params=_params(("arbitrary",)),
        name="qkv_rope",
    )(h, g, w_qkv, b_qkv, cos, sin_signed)


def _attn_kernel(sinks_ref, q_ref, k_ref, v_ref, k0_ref, v0_ref, h_ref, wo_ref, out_ref,
                 kprev_ref, vprev_ref, o_ref, *, tiles_per_seq, first_block_min_key):
    tq, d = q_ref.shape
    nblk = tq // BLOCK
    n_kv = d // (GROUP * HEAD_DIM)
    hw = GROUP * HEAD_DIM
    first = (pl.program_id(0) % tiles_per_seq) == 0

    @pl.when(first)
    def _():
        kprev_ref[...] = k0_ref[...]
        vprev_ref[...] = v0_ref[...]

    jmin = jnp.where(first, first_block_min_key, 0)
    rows = GROUP * BLOCK
    r_idx = lax.broadcasted_iota(jnp.int32, (rows, 2 * BLOCK), 0) % BLOCK
    j_idx = lax.broadcasted_iota(jnp.int32, (rows, 2 * BLOCK), 1)
    band = (j_idx > r_idx) & (j_idx <= r_idx + WINDOW)
    band_first = band & (j_idx >= jmin)
    row_grp = lax.broadcasted_iota(jnp.int32, (rows, 1), 0) // BLOCK
    lane_grp = lax.broadcasted_iota(jnp.int32, (BLOCK, hw), 1) // HEAD_DIM
    grp_mask = [lane_grp == g for g in range(GROUP)]
    grp_mask_bf = [m.astype(BF16) for m in grp_mask]

    for b in range(nblk):
        cur = slice(b * BLOCK, (b + 1) * BLOCK)
        if b == 0:
            kp, vp, valid = kprev_ref[...], vprev_ref[...], band_first
        else:
            prev = slice((b - 1) * BLOCK, b * BLOCK)
            kp, vp, valid = k_ref[prev, :], v_ref[prev, :], band
        kw = jnp.concatenate([kp, k_ref[cur, :]], axis=0)
        vw = jnp.concatenate([vp, v_ref[cur, :]], axis=0)
        qb = q_ref[cur, :]
        outs = []
        for hh in range(n_kv):
            ls = slice(hh * hw, (hh + 1) * hw)
            q_h, kw_h, vw_h = qb[:, ls], kw[:, ls], vw[:, ls]
            qs = jnp.concatenate([q_h * grp_mask_bf[g] for g in range(GROUP)], axis=0)
            s = lax.dot_general(qs, kw_h, (((1,), (1,)), ((), ())), preferred_element_type=F32)
            s = jnp.where(valid, s, NEG_BIG)
            sink = jnp.zeros((rows, 1), F32)
            for g in range(GROUP):
                sink = jnp.where(row_grp == g, sinks_ref[hh * GROUP + g] * LOG2_E, sink)
            m = jnp.maximum(jnp.max(s, axis=-1, keepdims=True), sink)
            p = jnp.exp2(s - m)
            denom = jnp.sum(p, axis=-1, keepdims=True) + jnp.exp2(sink - m)
            inv = 1.0 / denom
            pb = p.astype(BF16)
            o_h = jnp.zeros((BLOCK, hw), F32)
            for g in range(GROUP):
                rs = slice(g * BLOCK, (g + 1) * BLOCK)
                o_h = jnp.where(grp_mask[g], _dot(pb[rs], vw_h) * inv[rs], o_h)
            outs.append(o_h)
        o_ref[cur, :] = jnp.concatenate(outs, axis=1).astype(BF16)

    last = slice((nblk - 1) * BLOCK, nblk * BLOCK)
    kprev_ref[...] = k_ref[last, :]
    vprev_ref[...] = v_ref[last, :]
    out_ref[...] = h_ref[...] + _dot(o_ref[...], wo_ref[...])


def _attention(sinks, q, k, v, k0, v0, h, w_o, *, tq, tiles_per_seq, first_block_min_key):
    t, d = h.shape
    tok = pl.BlockSpec((tq, d), lambda i: (i, 0))
    return pl.pallas_call(
        functools.partial(_attn_kernel, tiles_per_seq=tiles_per_seq,
                          first_block_min_key=first_block_min_key),
        grid=(t // tq,),
        in_specs=[
            pl.BlockSpec(memory_space=pltpu.SMEM),
            tok, tok, tok,
            _const_spec((BLOCK, d)), _const_spec((BLOCK, d)),
            tok,
            _const_spec((d, d)),
        ],
        out_specs=tok,
        out_shape=jax.ShapeDtypeStruct((t, d), F32),
        scratch_shapes=[pltpu.VMEM((BLOCK, d), BF16), pltpu.VMEM((BLOCK, d), BF16),
                        pltpu.VMEM((tq, d), BF16)],
        compiler_params=_params(("arbitrary",)),
        name="swa_attention",
    )(sinks, q, k, v, k0, v0, h, w_o)


def _to_row_tiles(dst_ref, x):
    tm = x.shape[0]
    for c in range(x.shape[1] // V7X_LANES):
        dst_ref[pl.ds(c, tm, stride=V7X_SUBLANES), :] = x[:, c * V7X_LANES:(c + 1) * V7X_LANES]


def _from_row_tiles(src_ref, tm):
    n = src_ref.shape[0] // tm
    return jnp.concatenate([src_ref[pl.ds(c, tm, stride=n), :] for c in range(n)], axis=1)


def _router_kernel(h_ref, g_ref, wr_ref, xt_ref, route_ref, ids_ref):
    rt = h_ref.shape[1] // V7X_LANES
    w = wr_ref[...]
    w_hi = w.astype(BF16)
    w_lo = (w - w_hi.astype(F32)).astype(BF16)
    for rows in _row_pieces(h_ref.shape[0], 4):
        xn = _rmsnorm(h_ref[rows, :], g_ref[...])
        x_hi = xn.astype(BF16)
        x_lo = (xn - x_hi.astype(F32)).astype(BF16)
        logits = _dot(x_hi, w_hi) + (_dot(x_hi, w_lo) + _dot(x_lo, w_hi))
        lane = lax.broadcasted_iota(jnp.int32, logits.shape, 1)
        neg_inf = jnp.float32(-jnp.inf)
        lg = jnp.where(lane < N_EXPERTS, logits, neg_inf)
        m1 = jnp.max(lg, axis=-1, keepdims=True)
        i1 = jnp.min(jnp.where(lg == m1, lane, V7X_LANES), axis=-1, keepdims=True)
        lg2 = jnp.where(lane == i1, neg_inf, lg)
        m2 = jnp.max(lg2, axis=-1, keepdims=True)
        i2 = jnp.min(jnp.where(lg2 == m2, lane, V7X_LANES), axis=-1, keepdims=True)
        e = jnp.exp(m2 - m1)
        g1 = 1.0 / (1.0 + e)
        g2 = e / (1.0 + e)
        route = jnp.where(lane == 0, i1.astype(F32),
                          jnp.where(lane == 1, i2.astype(F32),
                                    jnp.where(lane == 2, g1, jnp.where(lane == 3, g2, 0.0))))
        route_ref[rows, :] = route
        ids_ref[:, rows] = route.T[0:V7X_SUBLANES, :].astype(jnp.int32)
        _to_row_tiles(xt_ref.at[pl.ds(rows.start * rt, (rows.stop - rows.start) * rt), :], xn)


def _router(h, g, w_router_padded, *, tm):
    t, d = h.shape
    rt = d // V7X_LANES
    return pl.pallas_call(
        _router_kernel,
        grid=(t // tm,),
        in_specs=[pl.BlockSpec((tm, d), lambda i: (i, 0)), _const_spec((1, d)),
                  _const_spec((d, V7X_LANES))],
        out_specs=[pl.BlockSpec((tm * rt, V7X_LANES), lambda i: (i, 0)),
                   pl.BlockSpec((tm, V7X_LANES), lambda i: (i, 0)),
                   pl.BlockSpec((V7X_SUBLANES, tm), lambda i: (i, 0))],
        out_shape=[jax.ShapeDtypeStruct((t * rt, V7X_LANES), F32),
                   jax.ShapeDtypeStruct((t, V7X_LANES), F32),
                   jax.ShapeDtypeStruct((t // tm * V7X_SUBLANES, tm), jnp.int32)],
        compiler_params=_params(("arbitrary",)),
        name="moe_router",
    )(h, g, w_router_padded)


def _row_copy(src_ref, src_row, dst_ref, dst_row, sem, rt):
    return pltpu.make_async_copy(src_ref.at[pl.ds(pl.multiple_of(src_row * rt, rt), rt), :],
                                 dst_ref.at[pl.ds(pl.multiple_of(dst_row * rt, rt), rt), :], sem)


def _wait_rows(hbm_ref, vmem_ref, sem):
    pltpu.make_async_copy(hbm_ref.at[pl.ds(0, vmem_ref.shape[0]), :], vmem_ref, sem).wait()


def _expert_kernel(w_tile_ref, w_expert_ref, w_first_ref, w_last_ref, w_lo_ref, w_hi_ref, n_work_ref,
                   idx_ref, xt_hbm, wg_ref, wu_ref, wd_ref, y_hbm,
                   gbuf_ref, xb_ref, acc_ref, stage_ref, gsem, ssem, *, chunk):
    del w_tile_ref, w_expert_ref, w_last_ref
    tm, d = xb_ref.shape
    rt = d // V7X_LANES
    w = pl.program_id(0)
    n_work = n_work_ref[0]
    active = w < n_work
    first = active & (w_first_ref[w] == 1)
    row_id = lax.broadcasted_iota(jnp.int32, (tm, 1), 0)

    def masked(part):
        return jnp.where((row_id >= w_lo_ref[w]) & (row_id < w_hi_ref[w]), part, 0.0)

    @pl.when(w == 0)
    def _():
        stage_ref[...] = jnp.zeros_like(stage_ref)
        acc_ref[...] = jnp.zeros_like(acc_ref)

        def body(r, c):
            _row_copy(xt_hbm, idx_ref[3 * tm + r], gbuf_ref, r, gsem, rt).start()
            _row_copy(stage_ref, r, y_hbm, idx_ref[tm + r], ssem, rt).start()
            return c
        lax.fori_loop(0, tm, body, 0)

    @pl.when(first)
    def _():
        _wait_rows(xt_hbm, gbuf_ref, gsem)
        xb_ref[...] = _from_row_tiles(gbuf_ref, tm).astype(BF16)

    def side_work(i, n):
        if i == 0:
            for r in range(tm):
                _row_copy(xt_hbm, idx_ref[r], gbuf_ref, r, gsem, rt).start()
            _wait_rows(y_hbm, stage_ref, ssem)
            _to_row_tiles(stage_ref, acc_ref[...])
        elif i == n // 2 - 1:
            for r in range(tm):
                _row_copy(stage_ref, r, y_hbm, idx_ref[tm + r], ssem, rt).start()

    @pl.when(first)
    def _():
        acc_ref[...] = masked(_swiglu_partial(xb_ref[...], wg_ref, wu_ref, wd_ref, chunk, side_work))

    @pl.when(active & jnp.logical_not(first))
    def _():
        acc_ref[...] += masked(_swiglu_partial(xb_ref[...], wg_ref, wu_ref, wd_ref, chunk))

    @pl.when(w == n_work - 1)
    def _():
        _wait_rows(y_hbm, stage_ref, ssem)
        _to_row_tiles(stage_ref, acc_ref[...])

        def body(r, c):
            _row_copy(stage_ref, r, y_hbm, idx_ref[2 * tm + r], ssem, rt).start()
            return c
        lax.fori_loop(0, tm, body, 0)
        _wait_rows(y_hbm, stage_ref, ssem)
        _wait_rows(xt_hbm, gbuf_ref, gsem)


def _experts(plan, sorted_a, xt, wg, wu, wd, *, tm, chunk=512):
    n_e, d, f = wg.shape
    rt = d // V7X_LANES
    n_assign = sorted_a.shape[0]
    n_tiles = n_assign // tm
    n_items = n_tiles + n_e - 1
    n_tok = n_assign // 2
    tok, k = sorted_a % n_tok, sorted_a // n_tok
    dst = (2 * tok + k).reshape(n_tiles, tm)
    src = tok.reshape(n_tiles, tm)
    dump = (n_assign + jnp.arange(tm, dtype=jnp.int32)).reshape(1, tm)
    idx = jnp.concatenate([jnp.concatenate([src[1:], jnp.zeros((1, tm), jnp.int32)], axis=0),
                           jnp.concatenate([dump, dst[:-1]], axis=0), dst, src], axis=1).reshape(-1)
    grid_spec = pltpu.PrefetchScalarGridSpec(
        num_scalar_prefetch=7,
        grid=(n_items,),
        in_specs=[
            pl.BlockSpec((4 * tm,), lambda w, wt, *_: (wt[w],), memory_space=pltpu.SMEM),
            pl.BlockSpec(memory_space=pl.ANY),
            pl.BlockSpec((None, d, f), lambda w, wt, we, *_: (we[w], 0, 0)),
            pl.BlockSpec((None, d, f), lambda w, wt, we, *_: (we[w], 0, 0)),
            pl.BlockSpec((None, f, d), lambda w, wt, we, *_: (we[w], 0, 0)),
        ],
        out_specs=pl.BlockSpec(memory_space=pl.ANY),
        scratch_shapes=[pltpu.VMEM((tm * rt, V7X_LANES), F32), pltpu.VMEM((tm, d), BF16),
                        pltpu.VMEM((tm, d), F32), pltpu.VMEM((tm * rt, V7X_LANES), F32),
                        pltpu.SemaphoreType.DMA, pltpu.SemaphoreType.DMA],
    )
    return pl.pallas_call(
        functools.partial(_expert_kernel, chunk=chunk),
        grid_spec=grid_spec,
        out_shape=jax.ShapeDtypeStruct(((n_assign + tm) * rt, V7X_LANES), F32),
        compiler_params=_params(("arbitrary",)),
        name="moe_experts",
    )(*plan, idx, xt, wg, wu, wd)


def _combine_kernel(y_ref, route_ref, h_ref, g_ref, out_ref):
    tm, d = h_ref.shape
    rt = d // V7X_LANES
    route = route_ref[...]
    g1, g2 = route[:, 2:3], route[:, 3:4]
    y1 = jnp.concatenate([y_ref[pl.ds(c, tm, stride=2 * rt), :] for c in range(rt)], axis=1)
    y2 = jnp.concatenate([y_ref[pl.ds(rt + c, tm, stride=2 * rt), :] for c in range(rt)], axis=1)
    out_ref[...] = _rmsnorm(h_ref[...] + g1 * y1 + g2 * y2, g_ref[...])


def _combine(y, route, h, g, *, tm):
    t, d = h.shape
    rt = d // V7X_LANES
    return pl.pallas_call(
        _combine_kernel,
        grid=(t // tm,),
        in_specs=[pl.BlockSpec((tm * 2 * rt, V7X_LANES), lambda i: (i, 0)),
                  pl.BlockSpec((tm, V7X_LANES), lambda i: (i, 0)),
                  pl.BlockSpec((tm, d), lambda i: (i, 0)), _const_spec((1, d))],
        out_specs=pl.BlockSpec((tm, d), lambda i: (i, 0)),
        out_shape=jax.ShapeDtypeStruct((t, d), F32),
        compiler_params=_params(("arbitrary",)),
        name="moe_combine_norm",
    )(y, route, h, g)


def _routing_plan(e1, e2, tm):
    flat_e = jnp.concatenate([e1, e2])
    n_assign = flat_e.shape[0]
    ids = jnp.arange(n_assign, dtype=jnp.int32)
    _, sorted_a = lax.sort((flat_e, ids), num_keys=1, is_stable=True)
    experts = jnp.arange(N_EXPERTS, dtype=jnp.int32)
    dense_e = flat_e.reshape(-1, V7X_LANES)
    counts = jnp.sum((dense_e[None] == experts[:, None, None]).astype(jnp.int32), axis=(1, 2))
    seg_end = jnp.cumsum(counts)
    seg_start = seg_end - counts
    first_tile = seg_start // tm
    n_items = jnp.where(counts > 0, (seg_end - 1) // tm - first_tile + 1, 0)
    item_end = jnp.cumsum(n_items)
    item_start = item_end - n_items
    n_work = item_end[-1]
    max_items = n_assign // tm + N_EXPERTS - 1
    w = jnp.minimum(jnp.arange(max_items, dtype=jnp.int32), n_work - 1)
    w_expert = jnp.sum((w[:, None] >= item_end[None, :]).astype(jnp.int32), axis=1)
    pick = lambda v: jnp.sum(jnp.where(w_expert[:, None] == experts[None, :], v[None, :], 0), axis=1)
    w_tile = pick(first_tile) + (w - pick(item_start))
    w_lo = jnp.maximum(pick(seg_start) - w_tile * tm, 0)
    w_hi = jnp.minimum(pick(seg_end) - w_tile * tm, tm)
    prev_tile = jnp.concatenate([jnp.full((1,), -1, jnp.int32), w_tile[:-1]])
    next_tile = jnp.concatenate([w_tile[1:], jnp.full((1,), -1, jnp.int32)])
    is_last_item = jnp.arange(max_items) == n_work - 1
    w_first = (w_tile != prev_tile).astype(jnp.int32)
    w_last = ((w_tile != next_tile) | is_last_item).astype(jnp.int32)
    i32 = lambda v: v.astype(jnp.int32)
    plan = (i32(w_tile), i32(w_expert), w_first, w_last, i32(w_lo), i32(w_hi), i32(n_work).reshape(1))
    return plan, sorted_a


def _rope_tables(length):
    pos = jnp.arange(length, dtype=F32)
    inv = ROPE_THETA ** (-jnp.arange(0, HEAD_DIM, 2, dtype=F32) / HEAD_DIM)
    ang = pos[:, None] * inv[None, :]
    cos, sin = jnp.cos(ang), jnp.sin(ang)
    reps = V7X_LANES // HEAD_DIM
    cos_l = jnp.tile(jnp.concatenate([cos, cos], axis=1), (1, reps))
    sin_l = jnp.tile(jnp.concatenate([-sin, sin], axis=1), (1, reps))
    return cos_l, sin_l


def _tile_rows(seq):
    narrow, wide = 4 * BLOCK, 8 * BLOCK
    assert seq % wide == 0
    return narrow, wide


def kernel(x, meta_tokens, conv_norm, conv_w_in, conv_w, conv_w_out, ffn_norm, ffn_w_gate, ffn_w_up, ffn_w_down, attn_norm, attn_w_qkv, attn_b_qkv, attn_sinks, attn_w_o, moe_norm, moe_w_router, moe_w_gate, moe_w_up, moe_w_down, final_norm):
    bsz, seq, d = x.shape
    n_meta = meta_tokens.shape[0]
    depth = conv_norm.shape[0] + attn_norm.shape[0]
    assert depth == 2 and conv_norm.shape[0] == 1 and attn_norm.shape[0] == 1
    assert n_meta % V7X_SUBLANES == 0 and n_meta <= BLOCK and n_meta >= CONV_WIDTH - 1
    q_dim = d
    kv_dim = (attn_w_qkv.shape[2] - q_dim) // 2
    n_kv = kv_dim // HEAD_DIM
    assert q_dim == n_kv * GROUP * HEAD_DIM

    tm, tw = _tile_rows(seq)
    wide_tiles_per_seq = seq // tw
    t = bsz * seq
    row = lambda a: a.reshape(1, -1).astype(F32)

    h = x.reshape(t, d)
    hm = meta_tokens.astype(F32)

    w_in = conv_w_in[0].astype(BF16)
    w_out = conv_w_out[0].astype(BF16)
    zero_carry = jnp.zeros((V7X_SUBLANES, d), F32)
    hm, vtail_m, _ = _conv_mixer(hm, zero_carry, row(conv_norm[0]), w_in, conv_w[0], w_out,
                                 tm=n_meta, tiles_per_seq=1)
    h, _, (wg, wu, wd, w_qkv, w_o) = _conv_mixer(
        h, vtail_m, row(conv_norm[0]), w_in, conv_w[0], w_out, tm=tw, tiles_per_seq=wide_tiles_per_seq,
        cast=(ffn_w_gate[0], ffn_w_up[0], ffn_w_down[0], attn_w_qkv[0], attn_w_o[0]))
    hm, _ = _ffn(hm, row(ffn_norm[0]), wg, wu, wd, tm=n_meta)
    n_e, _, e_ff = moe_w_gate[0].shape
    h, (eg, eu, ed) = _ffn(h, row(ffn_norm[0]), wg, wu, wd, tm=tm,
                           cast=(moe_w_gate[0].reshape(n_e * d, e_ff), moe_w_up[0].reshape(n_e * d, e_ff),
                                 moe_w_down[0].reshape(n_e * e_ff, d)))

    b_qkv = row(attn_b_qkv[0])
    cos_l, sin_l = _rope_tables(n_meta + seq)
    _, km, vm = _qkv_rope(hm, row(attn_norm[0]), w_qkv, b_qkv, cos_l[:n_meta], sin_l[:n_meta],
                          tm=n_meta, tiles_per_seq=1)
    q, k, v = _qkv_rope(h, row(attn_norm[0]), w_qkv, b_qkv, cos_l[n_meta:], sin_l[n_meta:],
                        tm=tw, tiles_per_seq=wide_tiles_per_seq)
    k0 = jnp.zeros((BLOCK, d), BF16).at[BLOCK - n_meta:].set(km)
    v0 = jnp.zeros((BLOCK, d), BF16).at[BLOCK - n_meta:].set(vm)
    h = _attention(attn_sinks[0].astype(F32), q, k, v, k0, v0, h, w_o,
                   tq=tw, tiles_per_seq=wide_tiles_per_seq, first_block_min_key=BLOCK - n_meta)

    w_r = jnp.zeros((d, V7X_LANES), F32).at[:, :N_EXPERTS].set(moe_w_router[0].astype(F32))
    xt, route, ids = _router(h, row(moe_norm[0]), w_r, tm=tw)
    ids = ids.reshape(t // tw, V7X_SUBLANES, tw)
    e1 = ids[:, 0, :].reshape(-1)
    e2 = ids[:, 1, :].reshape(-1)
    plan, sorted_a = _routing_plan(e1, e2, tm)
    y = _experts(plan, sorted_a, xt, eg.reshape(n_e, d, e_ff), eu.reshape(n_e, d, e_ff),
                 ed.reshape(n_e, e_ff, d), tm=tm)
    out = _combine(y, route, h, row(final_norm), tm=tw)
    return out.reshape(bsz, seq, d)
```

```python
import functools

import jax
import jax.numpy as jnp
from jax import lax
from jax.experimental import pallas as pl
from jax.experimental.pallas import tpu as pltpu

CONV_WIDTH = 3
HEAD_DIM = 64
GROUP = 4
WINDOW = 128
BLOCK = 128
ROPE_THETA = 10000.0
N_EXPERTS = 8
RMS_EPS = 1e-5
NEG_BIG = -1e30
LOG2_E = 1.4426950408889634

V7X_LANES = 128
V7X_SUBLANES = 8
VMEM_LIMIT = 56 * 1024 * 1024

BF16 = jnp.bfloat16
F32 = jnp.float32


def _dot(a, b):
    return jnp.dot(a, b, preferred_element_type=F32)


def _rmsnorm(x, g):
    ms = jnp.mean(x * x, axis=-1, keepdims=True)
    return x * lax.rsqrt(ms + RMS_EPS) * g


def _silu(g):
    return g * (1.0 / (1.0 + jnp.exp(-g)))


def _const_spec(shape):
    nd = len(shape)
    return pl.BlockSpec(shape, lambda *_: (0,) * nd, pipeline_mode=pl.Buffered(1))


def _params(sem):
    return pltpu.CompilerParams(dimension_semantics=sem, vmem_limit_bytes=VMEM_LIMIT)


def _row_pieces(tm, n):
    return [slice(i * (tm // n), (i + 1) * (tm // n)) for i in range(n)]


def _slab_spec(a, nt):
    ns = nt
    while a.shape[0] % ns or (a.shape[0] // ns) % (2 * V7X_SUBLANES):
        assert ns % 2 == 0
        ns //= 2
    steps = nt // ns
    return pl.BlockSpec((a.shape[0] // ns, a.shape[1]), lambda i: (i // steps, 0))


def _convert_slabs(cast_in, cast_out):
    for src, dst in zip(cast_in, cast_out):
        dst[...] = src[...].astype(BF16)


def _conv_mixer_kernel(h_ref, carry0_ref, g_ref, win_ref, cw_ref, wout_ref, *rest, tiles_per_seq, n_cast):
    cast_in, (out_ref, vtail_ref), cast_out = rest[:n_cast], rest[n_cast:n_cast + 2], rest[n_cast + 2:-1]
    carry_ref = rest[-1]
    _convert_slabs(cast_in, cast_out)
    d = h_ref.shape[1]
    tm = h_ref.shape[0]

    @pl.when(pl.program_id(0) % tiles_per_seq == 0)
    def _():
        carry_ref[...] = carry0_ref[...]

    h = h_ref[...]
    xn = _rmsnorm(h, g_ref[...]).astype(BF16)
    b_gate = _dot(xn, win_ref[:, 0:d])
    c_gate = _dot(xn, win_ref[:, d:2 * d])
    u = _dot(xn, win_ref[:, 2 * d:3 * d])
    v = c_gate * u
    carry = carry_ref[...]
    row = lax.broadcasted_iota(jnp.int32, (V7X_SUBLANES, d), 0)
    r1 = pltpu.roll(v, 1, axis=0)
    r2 = pltpu.roll(v, 2, axis=0)
    head1 = jnp.where(row < 1, pltpu.roll(carry, 1, axis=0), r1[0:V7X_SUBLANES])
    head2 = jnp.where(row < 2, pltpu.roll(carry, 2, axis=0), r2[0:V7X_SUBLANES])
    v1 = jnp.concatenate([head1, r1[V7X_SUBLANES:]], axis=0)
    v2 = jnp.concatenate([head2, r2[V7X_SUBLANES:]], axis=0)
    conv = cw_ref[0:1, :] * v2 + cw_ref[1:2, :] * v1 + cw_ref[2:3, :] * v
    y = (b_gate * conv).astype(BF16)
    out_ref[...] = h + _dot(y, wout_ref[...])
    tail = v[tm - V7X_SUBLANES:tm]
    carry_ref[...] = tail
    vtail_ref[...] = tail


def _conv_mixer(h, carry0, g, w_in, conv_w, w_out, *, tm, tiles_per_seq, cast=()):
    t, d = h.shape
    nt = t // tm
    outs = pl.pallas_call(
        functools.partial(_conv_mixer_kernel, tiles_per_seq=tiles_per_seq, n_cast=len(cast)),
        grid=(nt,),
        in_specs=[
            pl.BlockSpec((tm, d), lambda i: (i, 0)),
            _const_spec((V7X_SUBLANES, d)),
            _const_spec((1, d)),
            _const_spec((d, 3 * d)),
            _const_spec((CONV_WIDTH, d)),
            _const_spec((d, d)),
        ] + [_slab_spec(a, nt) for a in cast],
        out_specs=[
            pl.BlockSpec((tm, d), lambda i: (i, 0)),
            pl.BlockSpec((V7X_SUBLANES, d), lambda i: (i, 0)),
        ] + [_slab_spec(a, nt) for a in cast],
        out_shape=[
            jax.ShapeDtypeStruct((t, d), F32),
            jax.ShapeDtypeStruct((nt * V7X_SUBLANES, d), F32),
        ] + [jax.ShapeDtypeStruct(a.shape, BF16) for a in cast],
        scratch_shapes=[pltpu.VMEM((V7X_SUBLANES, d), F32)],
        compiler_params=_params(("arbitrary",)),
        name="conv_mixer",
    )(h, carry0, g, w_in, conv_w, w_out, *cast)
    return outs[0], outs[1], outs[2:]


def _ff_chunks(total, chunk):
    out, s = [], 0
    while s < total:
        out.append((s, min(chunk, total - s)))
        s += chunk
    return out


def _swiglu_partial(x, wg_ref, wu_ref, wd_ref, chunk, after_chunk=None):
    acc = None
    pieces = _ff_chunks(wg_ref.shape[1], chunk)
    for i, (s, n) in enumerate(pieces):
        g = _dot(x, wg_ref[:, s:s + n])
        u = _dot(x, wu_ref[:, s:s + n])
        a = (_silu(g) * u).astype(BF16)
        part = _dot(a, wd_ref[s:s + n, :])
        acc = part if acc is None else acc + part
        if after_chunk is not None:
            after_chunk(i, len(pieces))
    return acc


def _ffn_kernel(h_ref, g_ref, wg_ref, wu_ref, wd_ref, *rest, chunk, n_cast):
    cast_in, out_ref, cast_out = rest[:n_cast], rest[n_cast], rest[n_cast + 1:]
    h = h_ref[...]
    xn = _rmsnorm(h, g_ref[...]).astype(BF16)
    out_ref[...] = h + _swiglu_partial(xn, wg_ref, wu_ref, wd_ref, chunk)
    _convert_slabs(cast_in, cast_out)


def _ffn(h, g, wg, wu, wd, *, tm, chunk=512, cast=()):
    t, d = h.shape
    f = wg.shape[1]
    nt = t // tm
    slab = lambda a: _slab_spec(a, nt)
    outs = pl.pallas_call(
        functools.partial(_ffn_kernel, chunk=chunk, n_cast=len(cast)),
        grid=(nt,),
        in_specs=[
            pl.BlockSpec((tm, d), lambda i: (i, 0)),
            _const_spec((1, d)),
            _const_spec((d, f)),
            _const_spec((d, f)),
            _const_spec((f, d)),
        ] + [slab(a) for a in cast],
        out_specs=[pl.BlockSpec((tm, d), lambda i: (i, 0))] + [slab(a) for a in cast],
        out_shape=[jax.ShapeDtypeStruct((t, d), F32)] + [jax.ShapeDtypeStruct(a.shape, BF16) for a in cast],
        compiler_params=_params(("arbitrary",)),
        name="dense_ffn",
    )(h, g, wg, wu, wd, *cast)
    return outs[0], outs[1:]


def _rope(x, cos, sin_signed, first_half):
    outs = []
    for j in range(x.shape[1] // V7X_LANES):
        xj = x[:, j * V7X_LANES:(j + 1) * V7X_LANES]
        partner = jnp.where(first_half, pltpu.roll(xj, V7X_LANES - HEAD_DIM // 2, axis=1),
                            pltpu.roll(xj, HEAD_DIM // 2, axis=1))
        outs.append(xj * cos + partner * sin_signed)
    return jnp.concatenate(outs, axis=1)


def _repeat_heads(x, lane):
    half = V7X_LANES // 2
    assert HEAD_DIM == half and GROUP * HEAD_DIM == 2 * V7X_LANES
    outs = []
    for j in range(x.shape[1] // V7X_LANES):
        xj = x[:, j * V7X_LANES:(j + 1) * V7X_LANES]
        swapped = pltpu.roll(xj, half, axis=1)
        even = jnp.where(lane < half, xj, swapped)
        odd = jnp.where(lane < half, swapped, xj)
        outs += [even, even, odd, odd]
    return jnp.concatenate(outs, axis=1)


def _project_qkv(h, g_ref, w_ref, b_ref, cos, sin_signed, d):
    xn = _rmsnorm(h, g_ref[...]).astype(BF16)
    lane = lax.broadcasted_iota(jnp.int32, cos.shape, 1)
    first_half = (lane % HEAD_DIM) < (HEAD_DIM // 2)
    kvw = (w_ref.shape[1] - d) // 2
    q = _dot(xn, w_ref[:, 0:d]) + b_ref[:, 0:d]
    k = _dot(xn, w_ref[:, d:d + kvw]) + b_ref[:, d:d + kvw]
    v = _dot(xn, w_ref[:, d + kvw:d + 2 * kvw]) + b_ref[:, d + kvw:d + 2 * kvw]
    q = (_rope(q, cos, sin_signed, first_half) * (LOG2_E * HEAD_DIM ** -0.5)).astype(BF16)
    k = _repeat_heads(_rope(k, cos, sin_signed, first_half), lane).astype(BF16)
    v = _repeat_heads(v, lane).astype(BF16)
    return q, k, v


def _qkv_kernel(h_ref, g_ref, w_ref, b_ref, cos_ref, sin_ref, q_ref, k_ref, v_ref):
    q_ref[...], k_ref[...], v_ref[...] = _project_qkv(h_ref[...], g_ref, w_ref, b_ref, cos_ref[...],
                                                      sin_ref[...], h_ref.shape[1])


def _qkv_rope(h, g, w_qkv, b_qkv, cos, sin_signed, *, tm, tiles_per_seq):
    t, d = h.shape
    n = w_qkv.shape[1]
    tok = pl.BlockSpec((tm, d), lambda i: (i, 0))
    tab = pl.BlockSpec((tm, V7X_LANES), lambda i: (i % tiles_per_seq, 0))
    return pl.pallas_call(
        _qkv_kernel,
        grid=(t // tm,),
        in_specs=[tok, _const_spec((1, d)), _const_spec((d, n)), _const_spec((1, n)), tab, tab],
        out_specs=[tok, tok, tok],
        out_shape=[jax.ShapeDtypeStruct((t, d), BF16)] * 3,
        compiler_params=_params(("arbitrary",)),
        name="qkv_rope",
    )(h, g, w_qkv, b_qkv, cos, sin_signed)


def _attn_kernel(sinks_ref, q_ref, k_ref, v_ref, k0_ref, v0_ref, h_ref, wo_ref, out_ref,
                 kprev_ref, vprev_ref, o_ref, *, tiles_per_seq, first_block_min_key):
    tq, d = q_ref.shape
    nblk = tq // BLOCK
    n_kv = d // (GROUP * HEAD_DIM)
    hw = GROUP * HEAD_DIM
    first = (pl.program_id(0) % tiles_per_seq) == 0

    @pl.when(first)
    def _():
        kprev_ref[...] = k0_ref[...]
        vprev_ref[...] = v0_ref[...]

    jmin = jnp.where(first, first_block_min_key, 0)
    rows = GROUP * BLOCK
    r_idx = lax.broadcasted_iota(jnp.int32, (rows, 2 * BLOCK), 0) % BLOCK
    j_idx = lax.broadcasted_iota(jnp.int32, (rows, 2 * BLOCK), 1)
    band = (j_idx > r_idx) & (j_idx <= r_idx + WINDOW)
    band_first = band & (j_idx >= jmin)
    row_grp = lax.broadcasted_iota(jnp.int32, (rows, 1), 0) // BLOCK
    lane_grp = lax.broadcasted_iota(jnp.int32, (BLOCK, hw), 1) // HEAD_DIM
    grp_mask = [lane_grp == g for g in range(GROUP)]
    grp_mask_bf = [m.astype(BF16) for m in grp_mask]

    for b in range(nblk):
        cur = slice(b * BLOCK, (b + 1) * BLOCK)
        if b == 0:
            kp, vp, valid = kprev_ref[...], vprev_ref[...], band_first
        else:
            prev = slice((b - 1) * BLOCK, b * BLOCK)
            kp, vp, valid = k_ref[prev, :], v_ref[prev, :], band
        kw = jnp.concatenate([kp, k_ref[cur, :]], axis=0)
        vw = jnp.concatenate([vp, v_ref[cur, :]], axis=0)
        qb = q_ref[cur, :]
        outs = []
        for hh in range(n_kv):
            ls = slice(hh * hw, (hh + 1) * hw)
            q_h, kw_h, vw_h = qb[:, ls], kw[:, ls], vw[:, ls]
            qs = jnp.concatenate([q_h * grp_mask_bf[g] for g in range(GROUP)], axis=0)
            s = lax.dot_general(qs, kw_h, (((1,), (1,)), ((), ())), preferred_element_type=F32)
            s = jnp.where(valid, s, NEG_BIG)
            sink = jnp.zeros((rows, 1), F32)
            for g in range(GROUP):
                sink = jnp.where(row_grp == g, sinks_ref[hh * GROUP + g] * LOG2_E, sink)
            m = jnp.maximum(jnp.max(s, axis=-1, keepdims=True), sink)
            p = jnp.exp2(s - m)
            denom = jnp.sum(p, axis=-1, keepdims=True) + jnp.exp2(sink - m)
            inv = 1.0 / denom
            pb = p.astype(BF16)
            o_h = jnp.zeros((BLOCK, hw), F32)
            for g in range(GROUP):
                rs = slice(g * BLOCK, (g + 1) * BLOCK)
                o_h = jnp.where(grp_mask[g], _dot(pb[rs], vw_h) * inv[rs], o_h)
            outs.append(o_h)
        o_ref[cur, :] = jnp.concatenate(outs, axis=1).astype(BF16)

    last = slice((nblk - 1) * BLOCK, nblk * BLOCK)
    kprev_ref[...] = k_ref[last, :]
    vprev_ref[...] = v_ref[last, :]
    out_ref[...] = h_ref[...] + _dot(o_ref[...], wo_ref[...])


def _attention(sinks, q, k, v, k0, v0, h, w_o, *, tq, tiles_per_seq, first_block_min_key):
    t, d = h.shape
    tok = pl.BlockSpec((tq, d), lambda i: (i, 0))
    return pl.pallas_call(
        functools.partial(_attn_kernel, tiles_per_seq=tiles_per_seq,
                          first_block_min_key=first_block_min_key),
        grid=(t // tq,),
        in_specs=[
            pl.BlockSpec(memory_space=pltpu.SMEM),
            tok, tok, tok,
            _const_spec((BLOCK, d)), _const_spec((BLOCK, d)),
            tok,
            _const_spec((d, d)),
        ],
        out_specs=tok,
        out_shape=jax.ShapeDtypeStruct((t, d), F32),
        scratch_shapes=[pltpu.VMEM((BLOCK, d), BF16), pltpu.VMEM((BLOCK, d), BF16),
                        pltpu.VMEM((tq, d), BF16)],
        compiler_params=_params(("arbitrary",)),
        name="swa_attention",
    )(sinks, q, k, v, k0, v0, h, w_o)


def _to_row_tiles(dst_ref, x):
    tm = x.shape[0]
    for c in range(x.shape[1] // V7X_LANES):
        dst_ref[pl.ds(c, tm, stride=V7X_SUBLANES), :] = x[:, c * V7X_LANES:(c + 1) * V7X_LANES]


def _from_row_tiles(src_ref, tm):
    n = src_ref.shape[0] // tm
    return jnp.concatenate([src_ref[pl.ds(c, tm, stride=n), :] for c in range(n)], axis=1)


def _router_kernel(h_ref, g_ref, wr_ref, xt_ref, route_ref, ids_ref):
    rt = h_ref.shape[1] // V7X_LANES
    w = wr_ref[...]
    w_hi = w.astype(BF16)
    w_lo = (w - w_hi.astype(F32)).astype(BF16)
    for rows in _row_pieces(h_ref.shape[0], 4):
        xn = _rmsnorm(h_ref[rows, :], g_ref[...])
        x_hi = xn.astype(BF16)
        x_lo = (xn - x_hi.astype(F32)).astype(BF16)
        logits = _dot(x_hi, w_hi) + (_dot(x_hi, w_lo) + _dot(x_lo, w_hi))
        lane = lax.broadcasted_iota(jnp.int32, logits.shape, 1)
        neg_inf = jnp.float32(-jnp.inf)
        lg = jnp.where(lane < N_EXPERTS, logits, neg_inf)
        m1 = jnp.max(lg, axis=-1, keepdims=True)
        i1 = jnp.min(jnp.where(lg == m1, lane, V7X_LANES), axis=-1, keepdims=True)
        lg2 = jnp.where(lane == i1, neg_inf, lg)
        m2 = jnp.max(lg2, axis=-1, keepdims=True)
        i2 = jnp.min(jnp.where(lg2 == m2, lane, V7X_LANES), axis=-1, keepdims=True)
        e = jnp.exp(m2 - m1)
        g1 = 1.0 / (1.0 + e)
        g2 = e / (1.0 + e)
        route = jnp.where(lane == 0, i1.astype(F32),
                          jnp.where(lane == 1, i2.astype(F32),
                                    jnp.where(lane == 2, g1, jnp.where(lane == 3, g2, 0.0))))
        route_ref[rows, :] = route
        ids_ref[:, rows] = route.T[0:V7X_SUBLANES, :].astype(jnp.int32)
        _to_row_tiles(xt_ref.at[pl.ds(rows.start * rt, (rows.stop - rows.start) * rt), :], xn)


def _router(h, g, w_router_padded, *, tm):
    t, d = h.shape
    rt = d // V7X_LANES
    return pl.pallas_call(
        _router_kernel,
        grid=(t // tm,),
        in_specs=[pl.BlockSpec((tm, d), lambda i: (i, 0)), _const_spec((1, d)),
                  _const_spec((d, V7X_LANES))],
        out_specs=[pl.BlockSpec((tm * rt, V7X_LANES), lambda i: (i, 0)),
                   pl.BlockSpec((tm, V7X_LANES), lambda i: (i, 0)),
                   pl.BlockSpec((V7X_SUBLANES, tm), lambda i: (i, 0))],
        out_shape=[jax.ShapeDtypeStruct((t * rt, V7X_LANES), F32),
                   jax.ShapeDtypeStruct((t, V7X_LANES), F32),
                   jax.ShapeDtypeStruct((t // tm * V7X_SUBLANES, tm), jnp.int32)],
        compiler_params=_params(("arbitrary",)),
        name="moe_router",
    )(h, g, w_router_padded)


def _row_copy(src_ref, src_row, dst_ref, dst_row, sem, rt):
    return pltpu.make_async_copy(src_ref.at[pl.ds(pl.multiple_of(src_row * rt, rt), rt), :],
                                 dst_ref.at[pl.ds(pl.multiple_of(dst_row * rt, rt), rt), :], sem)


def _wait_rows(hbm_ref, vmem_ref, sem):
    pltpu.make_async_copy(hbm_ref.at[pl.ds(0, vmem_ref.shape[0]), :], vmem_ref, sem).wait()


def _expert_kernel(w_tile_ref, w_expert_ref, w_first_ref, w_last_ref, w_lo_ref, w_hi_ref, n_work_ref,
                   idx_ref, xt_hbm, wg_ref, wu_ref, wd_ref, y_hbm,
                   gbuf_ref, xb_ref, acc_ref, stage_ref, gsem, ssem, *, chunk):
    del w_expert_ref, w_last_ref
    _, tm, d = xb_ref.shape
    rt = d // V7X_LANES
    w = pl.program_id(0)
    n_work = n_work_ref[0]
    active = w < n_work
    first = active & (w_first_ref[w] == 1)
    slot = w_tile_ref[w] % 2
    row_id = lax.broadcasted_iota(jnp.int32, (tm, 1), 0)

    def masked(part):
        return jnp.where((row_id >= w_lo_ref[w]) & (row_id < w_hi_ref[w]), part, 0.0)

    @pl.when(w == 0)
    def _():
        stage_ref[...] = jnp.zeros_like(stage_ref)
        acc_ref[...] = jnp.zeros_like(acc_ref)

        def body(r, c):
            _row_copy(xt_hbm, idx_ref[3 * tm + r], gbuf_ref, r, gsem, rt).start()
            _row_copy(stage_ref, r, y_hbm, idx_ref[tm + r], ssem, rt).start()
            return c
        lax.fori_loop(0, tm, body, 0)
        _wait_rows(xt_hbm, gbuf_ref, gsem)
        xb_ref[slot] = _from_row_tiles(gbuf_ref, tm).astype(BF16)

    def side_work(i, n):
        if i == 0:
            for r in range(tm):
                _row_copy(xt_hbm, idx_ref[r], gbuf_ref, r, gsem, rt).start()
            _wait_rows(y_hbm, stage_ref, ssem)
            _to_row_tiles(stage_ref, acc_ref[...])
        elif i == n // 2 - 1:
            for r in range(tm):
                _row_copy(stage_ref, r, y_hbm, idx_ref[tm + r], ssem, rt).start()
        elif i == n - 2:
            _wait_rows(xt_hbm, gbuf_ref, gsem)
            xb_ref[1 - slot] = _from_row_tiles(gbuf_ref, tm).astype(BF16)

    @pl.when(first)
    def _():
        acc_ref[...] = masked(_swiglu_partial(xb_ref[slot], wg_ref, wu_ref, wd_ref, chunk, side_work))

    @pl.when(active & jnp.logical_not(first))
    def _():
        acc_ref[...] += masked(_swiglu_partial(xb_ref[slot], wg_ref, wu_ref, wd_ref, chunk))

    @pl.when(w == n_work - 1)
    def _():
        _wait_rows(y_hbm, stage_ref, ssem)
        _to_row_tiles(stage_ref, acc_ref[...])

        def body(r, c):
            _row_copy(stage_ref, r, y_hbm, idx_ref[2 * tm + r], ssem, rt).start()
            return c
        lax.fori_loop(0, tm, body, 0)
        _wait_rows(y_hbm, stage_ref, ssem)


def _experts(plan, sorted_a, xt, wg, wu, wd, *, tm, chunk=512):
    n_e, d, f = wg.shape
    rt = d // V7X_LANES
    n_assign = sorted_a.shape[0]
    n_tiles = n_assign // tm
    n_items = n_tiles + n_e - 1
    n_tok = n_assign // 2
    tok, k = sorted_a % n_tok, sorted_a // n_tok
    dst = (2 * tok + k).reshape(n_tiles, tm)
    src = tok.reshape(n_tiles, tm)
    dump = (n_assign + jnp.arange(tm, dtype=jnp.int32)).reshape(1, tm)
    idx = jnp.concatenate([jnp.concatenate([src[1:], jnp.zeros((1, tm), jnp.int32)], axis=0),
                           jnp.concatenate([dump, dst[:-1]], axis=0), dst, src], axis=1).reshape(-1)
    grid_spec = pltpu.PrefetchScalarGridSpec(
        num_scalar_prefetch=7,
        grid=(n_items,),
        in_specs=[
            pl.BlockSpec((4 * tm,), lambda w, wt, *_: (wt[w],), memory_space=pltpu.SMEM),
            pl.BlockSpec(memory_space=pl.ANY),
            pl.BlockSpec((None, d, f), lambda w, wt, we, *_: (we[w], 0, 0)),
            pl.BlockSpec((None, d, f), lambda w, wt, we, *_: (we[w], 0, 0)),
            pl.BlockSpec((None, f, d), lambda w, wt, we, *_: (we[w], 0, 0), pipeline_mode=pl.Buffered(1)),
        ],
        out_specs=pl.BlockSpec(memory_space=pl.ANY),
        scratch_shapes=[pltpu.VMEM((tm * rt, V7X_LANES), F32), pltpu.VMEM((2, tm, d), BF16),
                        pltpu.VMEM((tm, d), F32), pltpu.VMEM((tm * rt, V7X_LANES), F32),
                        pltpu.SemaphoreType.DMA, pltpu.SemaphoreType.DMA],
    )
    return pl.pallas_call(
        functools.partial(_expert_kernel, chunk=chunk),
        grid_spec=grid_spec,
        out_shape=jax.ShapeDtypeStruct(((n_assign + tm) * rt, V7X_LANES), F32),
        compiler_params=_params(("arbitrary",)),
        name="moe_experts",
    )(*plan, idx, xt, wg, wu, wd)


def _combine_kernel(y_ref, route_ref, h_ref, g_ref, out_ref):
    tm, d = h_ref.shape
    rt = d // V7X_LANES
    route = route_ref[...]
    g1, g2 = route[:, 2:3], route[:, 3:4]
    y1 = jnp.concatenate([y_ref[pl.ds(c, tm, stride=2 * rt), :] for c in range(rt)], axis=1)
    y2 = jnp.concatenate([y_ref[pl.ds(rt + c, tm, stride=2 * rt), :] for c in range(rt)], axis=1)
    out_ref[...] = _rmsnorm(h_ref[...] + g1 * y1 + g2 * y2, g_ref[...])


def _combine(y, route, h, g, *, tm):
    t, d = h.shape
    rt = d // V7X_LANES
    return pl.pallas_call(
        _combine_kernel,
        grid=(t // tm,),
        in_specs=[pl.BlockSpec((tm * 2 * rt, V7X_LANES), lambda i: (i, 0)),
                  pl.BlockSpec((tm, V7X_LANES), lambda i: (i, 0)),
                  pl.BlockSpec((tm, d), lambda i: (i, 0)), _const_spec((1, d))],
        out_specs=pl.BlockSpec((tm, d), lambda i: (i, 0)),
        out_shape=jax.ShapeDtypeStruct((t, d), F32),
        compiler_params=_params(("arbitrary",)),
        name="moe_combine_norm",
    )(y, route, h, g)


def _routing_plan(e1, e2, tm):
    flat_e = jnp.concatenate([e1, e2])
    n_assign = flat_e.shape[0]
    ids = jnp.arange(n_assign, dtype=jnp.int32)
    _, sorted_a = lax.sort((flat_e, ids), num_keys=1, is_stable=True)
    experts = jnp.arange(N_EXPERTS, dtype=jnp.int32)
    dense_e = flat_e.reshape(-1, V7X_LANES)
    counts = jnp.sum((dense_e[None] == experts[:, None, None]).astype(jnp.int32), axis=(1, 2))
    seg_end = jnp.cumsum(counts)
    seg_start = seg_end - counts
    first_tile = seg_start // tm
    n_items = jnp.where(counts > 0, (seg_end - 1) // tm - first_tile + 1, 0)
    item_end = jnp.cumsum(n_items)
    item_start = item_end - n_items
    n_work = item_end[-1]
    max_items = n_assign // tm + N_EXPERTS - 1
    w = jnp.minimum(jnp.arange(max_items, dtype=jnp.int32), n_work - 1)
    w_expert = jnp.sum((w[:, None] >= item_end[None, :]).astype(jnp.int32), axis=1)
    pick = lambda v: jnp.sum(jnp.where(w_expert[:, None] == experts[None, :], v[None, :], 0), axis=1)
    w_tile = pick(first_tile) + (w - pick(item_start))
    w_lo = jnp.maximum(pick(seg_start) - w_tile * tm, 0)
    w_hi = jnp.minimum(pick(seg_end) - w_tile * tm, tm)
    prev_tile = jnp.concatenate([jnp.full((1,), -1, jnp.int32), w_tile[:-1]])
    next_tile = jnp.concatenate([w_tile[1:], jnp.full((1,), -1, jnp.int32)])
    is_last_item = jnp.arange(max_items) == n_work - 1
    w_first = (w_tile != prev_tile).astype(jnp.int32)
    w_last = ((w_tile != next_tile) | is_last_item).astype(jnp.int32)
    i32 = lambda v: v.astype(jnp.int32)
    plan = (i32(w_tile), i32(w_expert), w_first, w_last, i32(w_lo), i32(w_hi), i32(n_work).reshape(1))
    return plan, sorted_a


def _rope_tables(length):
    pos = jnp.arange(length, dtype=F32)
    inv = ROPE_THETA ** (-jnp.arange(0, HEAD_DIM, 2, dtype=F32) / HEAD_DIM)
    ang = pos[:, None] * inv[None, :]
    cos, sin = jnp.cos(ang), jnp.sin(ang)
    reps = V7X_LANES // HEAD_DIM
    cos_l = jnp.tile(jnp.concatenate([cos, cos], axis=1), (1, reps))
    sin_l = jnp.tile(jnp.concatenate([-sin, sin], axis=1), (1, reps))
    return cos_l, sin_l


def _tile_rows(seq):
    narrow, wide = 4 * BLOCK, 8 * BLOCK
    assert seq % wide == 0
    return narrow, wide


def kernel(x, meta_tokens, conv_norm, conv_w_in, conv_w, conv_w_out, ffn_norm, ffn_w_gate, ffn_w_up, ffn_w_down, attn_norm, attn_w_qkv, attn_b_qkv, attn_sinks, attn_w_o, moe_norm, moe_w_router, moe_w_gate, moe_w_up, moe_w_down, final_norm):
    bsz, seq, d = x.shape
    n_meta = meta_tokens.shape[0]
    depth = conv_norm.shape[0] + attn_norm.shape[0]
    assert depth == 2 and conv_norm.shape[0] == 1 and attn_norm.shape[0] == 1
    assert n_meta % V7X_SUBLANES == 0 and n_meta <= BLOCK and n_meta >= CONV_WIDTH - 1
    q_dim = d
    kv_dim = (attn_w_qkv.shape[2] - q_dim) // 2
    n_kv = kv_dim // HEAD_DIM
    assert q_dim == n_kv * GROUP * HEAD_DIM

    tm, tw = _tile_rows(seq)
    wide_tiles_per_seq = seq // tw
    t = bsz * seq
    row = lambda a: a.reshape(1, -1).astype(F32)

    h = x.reshape(t, d)
    hm = meta_tokens.astype(F32)

    w_in = conv_w_in[0].astype(BF16)
    w_out = conv_w_out[0].astype(BF16)
    zero_carry = jnp.zeros((V7X_SUBLANES, d), F32)
    hm, vtail_m, _ = _conv_mixer(hm, zero_carry, row(conv_norm[0]), w_in, conv_w[0], w_out,
                                 tm=n_meta, tiles_per_seq=1)
    h, _, (wg, wu, wd, w_qkv, w_o) = _conv_mixer(
        h, vtail_m, row(conv_norm[0]), w_in, conv_w[0], w_out, tm=tw, tiles_per_seq=wide_tiles_per_seq,
        cast=(ffn_w_gate[0], ffn_w_up[0], ffn_w_down[0], attn_w_qkv[0], attn_w_o[0]))
    hm, _ = _ffn(hm, row(ffn_norm[0]), wg, wu, wd, tm=n_meta)
    n_e, _, e_ff = moe_w_gate[0].shape
    h, (eg, eu, ed) = _ffn(h, row(ffn_norm[0]), wg, wu, wd, tm=tm,
                           cast=(moe_w_gate[0].reshape(n_e * d, e_ff), moe_w_up[0].reshape(n_e * d, e_ff),
                                 moe_w_down[0].reshape(n_e * e_ff, d)))

    b_qkv = row(attn_b_qkv[0])
    cos_l, sin_l = _rope_tables(n_meta + seq)
    _, km, vm = _qkv_rope(hm, row(attn_norm[0]), w_qkv, b_qkv, cos_l[:n_meta], sin_l[:n_meta],
                          tm=n_meta, tiles_per_seq=1)
    q, k, v = _qkv_rope(h, row(attn_norm[0]), w_qkv, b_qkv, cos_l[n_meta:], sin_l[n_meta:],
                        tm=tw, tiles_per_seq=wide_tiles_per_seq)
    k0 = jnp.zeros((BLOCK, d), BF16).at[BLOCK - n_meta:].set(km)
    v0 = jnp.zeros((BLOCK, d), BF16).at[BLOCK - n_meta:].set(vm)
    h = _attention(attn_sinks[0].astype(F32), q, k, v, k0, v0, h, w_o,
                   tq=tw, tiles_per_seq=wide_tiles_per_seq, first_block_min_key=BLOCK - n_meta)

    w_r = jnp.zeros((d, V7X_LANES), F32).at[:, :N_EXPERTS].set(moe_w_router[0].astype(F32))
    xt, route, ids = _router(h, row(moe_norm[0]), w_r, tm=tw)
    ids = ids.reshape(t // tw, V7X_SUBLANES, tw)
    e1 = ids[:, 0, :].reshape(-1)
    e2 = ids[:, 1, :].reshape(-1)
    plan, sorted_a = _routing_plan(e1, e2, tm)
    y = _experts(plan, sorted_a, xt, eg.reshape(n_e, d, e_ff), eu.reshape(n_e, d, e_ff),
                 ed.reshape(n_e, e_ff, d), tm=tm)
    out = _combine(y, route, h, row(final_norm), tm=tw)
    return out.reshape(bsz, seq, d)
```
